```python
import math
import jax, jax.numpy as jnp
from jax import lax
import numpy as np

D_MODEL = 2048
BATCH = 4
SEQ = 2048
DEPTH = 1

GRID_W = 64
CTX_LEN = 256
EPS = 1e-6
NEG_INF = -1e30
D_INNER = 2 * D_MODEL
SSM_HEADDIM = 64
SSM_HEADS = D_INNER // SSM_HEADDIM
SSM_GROUPS = 8
D_STATE = 128
D_CONV = 5
CHUNK = 128
HEAD_DIM = 128
ATTN_HEADS = D_MODEL // HEAD_DIM
KV_HEADS = 4
Q_PER_KV = ATTN_HEADS // KV_HEADS
WINDOW = 128
ATTN_BLOCK = 128
ROPE_BASE = 10000.0
ROPE_PAIRS = HEAD_DIM // 4
PEER_HEADS = 8
N_KEYS = 128
N_EXPERTS = N_KEYS * N_KEYS
PEER_KEY_DIM = 256
KEY_HALF = PEER_KEY_DIM // 2
PEER_TOPK = 16
TOKEN_BLOCK = 128
K_W = KV_HEADS * HEAD_DIM
GN_W = SSM_GROUPS * D_STATE
XBC_W = D_INNER + 2 * GN_W
DT_W = 2 * SSM_HEADS
Q_W = ATTN_HEADS * HEAD_DIM
Z_W = D_INNER
GATE_W = 2 * D_MODEL
CTX_COLS = 2 * K_W + XBC_W + DT_W
TOTAL_COLS = CTX_COLS + Q_W + Z_W + GATE_W

kernel_name = "hybrid_ssd_swa_peer_dit_block"


def rms_norm(t, g):
    tf = t.astype(jnp.float32)
    tf = tf * lax.rsqrt(jnp.mean(tf * tf, axis=-1, keepdims=True) + EPS)
    return tf.astype(t.dtype) * g


def modulate(t, shift, scale):
    return t * (1.0 + scale) + shift


def axial_rope_tables(rows):
    row = jnp.repeat(jnp.arange(rows), GRID_W).astype(jnp.float32)
    col = jnp.tile(jnp.arange(GRID_W), rows).astype(jnp.float32)
    freqs = ROPE_BASE ** (-jnp.arange(ROPE_PAIRS, dtype=jnp.float32) / ROPE_PAIRS)
    ar = row[:, None] * freqs
    ac = col[:, None] * freqs
    ang = jnp.concatenate([ar, ar, ac, ac], axis=-1)
    return jnp.cos(ang), jnp.sin(ang)


def rope2d(t, cos, sin):
    shp = (1, cos.shape[0]) + (1,) * (t.ndim - 3) + (HEAD_DIM,)
    a1, a2, b1, b2 = jnp.split(t, 4, axis=-1)
    rot = jnp.concatenate([-a2, a1, -b2, b1], axis=-1)
    return t * cos.reshape(shp).astype(t.dtype) + rot * sin.reshape(shp).astype(t.dtype)


def softmax_with_sink(s, sink):
    col = jnp.broadcast_to(sink.astype(jnp.float32)[:, :, None, None], s.shape[:-1] + (1,))
    p = jax.nn.softmax(jnp.concatenate([s, col], axis=-1), axis=-1)
    return p[..., :-1]


def banded_window_attention(q, k, v, kc, vc, sink):
    b, S = q.shape[:2]
    nb = S // ATTN_BLOCK
    scale = HEAD_DIM ** -0.5
    qb = q.reshape(b, nb, ATTN_BLOCK, KV_HEADS, Q_PER_KV, HEAD_DIM)

    def windows(t):
        tp = jnp.pad(t, ((0, 0), (ATTN_BLOCK, ATTN_BLOCK), (0, 0), (0, 0)))
        tb = tp.reshape(b, nb + 2, ATTN_BLOCK, KV_HEADS, HEAD_DIM)
        return jnp.concatenate([tb[:, :-2], tb[:, 1:-1], tb[:, 2:]], axis=2)

    kw, vw = windows(k), windows(v)
    s_loc = jnp.einsum('bnqhrd,bnkhd->bnhrqk', qb, kw).astype(jnp.float32) * scale
    k_rel = jnp.arange(3 * ATTN_BLOCK) - ATTN_BLOCK
    q_rel = jnp.arange(ATTN_BLOCK)
    band = jnp.abs(k_rel[None, :] - q_rel[:, None]) <= WINDOW
    k_abs = jnp.arange(nb)[:, None] * ATTN_BLOCK + k_rel[None, :]
    valid = (k_abs >= 0) & (k_abs < S)
    mask = band[None] & valid[:, None, :]
    s_loc = jnp.where(mask[None, :, None, None], s_loc, NEG_INF)
    s_ctx = jnp.einsum('bnqhrd,bkhd->bnhrqk', qb, kc).astype(jnp.float32) * scale
    p = softmax_with_sink(jnp.concatenate([s_loc, s_ctx], axis=-1), sink)
    p_loc = p[..., :3 * ATTN_BLOCK].astype(v.dtype)
    p_ctx = p[..., 3 * ATTN_BLOCK:].astype(v.dtype)
    out = jnp.einsum('bnhrqk,bnkhd->bnqhrd', p_loc, vw) + jnp.einsum('bnhrqk,bkhd->bnqhrd', p_ctx, vc)
    return out.reshape(b, S, Q_W)


def context_attention(qc, kc, vc, sink):
    b, C = qc.shape[:2]
    s = jnp.einsum('bqhrd,bkhd->bhrqk', qc, kc).astype(jnp.float32) * (HEAD_DIM ** -0.5)
    p = softmax_with_sink(s, sink).astype(vc.dtype)
    return jnp.einsum('bhrqk,bkhd->bqhrd', p, vc).reshape(b, C, Q_W)


def dwconv_silu(t, w, bias):
    ch = t.shape[-1]
    out = lax.conv_general_dilated(t, w[:, None, :].astype(t.dtype), window_strides=(1,),
                                   padding=[(D_CONV // 2, D_CONV // 2)],
                                   dimension_numbers=('NWC', 'WIO', 'NWC'), feature_group_count=ch)
    return jax.nn.silu(out + bias)


def ssd_chunked(xh, dt, A, Bm, Cm, h0, return_y):
    b, L, H, P = xh.shape
    G, N = Bm.shape[-2:]
    R = H // G
    nc = L // CHUNK
    xr = xh.astype(jnp.float32).reshape(b, nc, CHUNK, G, R, P)
    dtr = dt.reshape(b, nc, CHUNK, G, R)
    Br = Bm.astype(jnp.float32).reshape(b, nc, CHUNK, G, N)
    Cr = Cm.astype(jnp.float32).reshape(b, nc, CHUNK, G, N)
    acs = jnp.cumsum(dtr * A.reshape(G, R), axis=2)
    a_last = acs[:, :, -1]
    wx = dtr[..., None] * xr
    wx_end = jnp.exp(a_last[:, :, None] - acs)[..., None] * wx
    states = jnp.einsum('bcsgn,bcsgrp->bcgrpn', Br, wx_end)

    def step(h, inp):
        s_c, al = inp
        return jnp.exp(al)[..., None, None] * h + s_c, h

    h_final, h_prev = lax.scan(step, h0.astype(jnp.float32).reshape(b, G, R, P, N),
                               (jnp.moveaxis(states, 1, 0), jnp.moveaxis(a_last, 1, 0)))
    h_final = h_final.reshape(b, H, P, N)
    if not return_y:
        return None, h_final
    h_prev = jnp.moveaxis(h_prev, 0, 1)
    y_off = jnp.einsum('bclgn,bcgrpn->bclgrp', Cr, h_prev) * jnp.exp(acs)[..., None]
    causal = jnp.tril(jnp.ones((CHUNK, CHUNK), dtype=bool))
    seg = acs[:, :, :, None] - acs[:, :, None, :]
    decay = jnp.exp(jnp.where(causal[None, None, :, :, None, None], seg, -jnp.inf))
    CB = jnp.einsum('bclgn,bcsgn->bclsg', Cr, Br)
    y_diag = jnp.einsum('bclsgr,bcsgrp->bclgrp', CB[..., None] * decay, wx)
    return (y_diag + y_off).reshape(b, L, H, P), h_final


def ssm_bidir(xbc, dt_raw, dt_bias, a_log, h0_f, h0_b, return_y):
    b, L, _ = xbc.shape
    xh = xbc[..., :D_INNER].reshape(b, L, SSM_HEADS, SSM_HEADDIM)
    Bm = xbc[..., D_INNER:D_INNER + GN_W].reshape(b, L, SSM_GROUPS, D_STATE)
    Cm = xbc[..., D_INNER + GN_W:].reshape(b, L, SSM_GROUPS, D_STATE)
    A = -jnp.exp(a_log.astype(jnp.float32))
    dtf = dt_raw.astype(jnp.float32)
    dt_f = jax.nn.softplus(dtf[..., :SSM_HEADS] + dt_bias[0].astype(jnp.float32))
    dt_b = jax.nn.softplus(dtf[..., SSM_HEADS:] + dt_bias[1].astype(jnp.float32))
    fl = lambda t: jnp.flip(t, axis=1)
    y_f, h_f = ssd_chunked(xh, dt_f, A[0], Bm, Cm, h0_f, return_y)
    y_b, h_b = ssd_chunked(fl(xh), fl(dt_b), A[1], fl(Bm), fl(Cm), h0_b, return_y)
    y = y_f + fl(y_b) if return_y else None
    return y, xh, h_f, h_b


def merge_branches(y, xh, z, gates, attn, ssm_d, ssm_norm_g, w_br_ssm, w_br_attn, w_out):
    b, L = y.shape[:2]
    y = (y + ssm_d.astype(jnp.float32)[:, None] * xh.astype(jnp.float32)).reshape(b, L, D_INNER)
    y = rms_norm(y * jax.nn.silu(z.astype(jnp.float32)), ssm_norm_g).astype(z.dtype)
    g = jax.nn.sigmoid(gates)
    merged = g[..., :D_MODEL] * (y @ w_br_ssm) + g[..., D_MODEL:] * (attn @ w_br_attn)
    return merged @ w_out


def mixing_sublayer(u, uc, w_in, conv_w, conv_b, dt_bias, a_log, ssm_d, ssm_norm_g, attn_sink,
                    w_br_ssm, w_br_attn, w_out, cos, sin, ctx_out):
    b, S, _ = u.shape
    C = uc.shape[1]
    p = u @ w_in
    pc = uc @ (w_in if ctx_out else w_in[:, :CTX_COLS])

    def ctx_side(t):
        return (t[..., :K_W], t[..., K_W:2 * K_W], t[..., 2 * K_W:2 * K_W + XBC_W],
                t[..., 2 * K_W + XBC_W:CTX_COLS])

    def query_side(t):
        return (t[..., CTX_COLS:CTX_COLS + Q_W], t[..., CTX_COLS + Q_W:CTX_COLS + Q_W + Z_W],
                t[..., CTX_COLS + Q_W + Z_W:])

    k, v, xbc, dt = ctx_side(p)
    q, z, gates = query_side(p)
    kc, vc, xbcc, dtc = ctx_side(pc)
    sink = attn_sink.reshape(KV_HEADS, Q_PER_KV)
    q = rope2d(q.reshape(b, S, KV_HEADS, Q_PER_KV, HEAD_DIM), cos, sin)
    k = rope2d(k.reshape(b, S, KV_HEADS, HEAD_DIM), cos, sin)
    v = v.reshape(b, S, KV_HEADS, HEAD_DIM)
    kc = kc.reshape(b, C, KV_HEADS, HEAD_DIM)
    vc = vc.reshape(b, C, KV_HEADS, HEAD_DIM)
    attn = banded_window_attention(q, k, v, kc, vc, sink)
    xbc = dwconv_silu(xbc, conv_w, conv_b)
    xbcc = dwconv_silu(xbcc, conv_w, conv_b)
    h0 = jnp.zeros((b, SSM_HEADS, SSM_HEADDIM, D_STATE), jnp.float32)
    yc, xhc, hcf, hcb = ssm_bidir(xbcc, dtc, dt_bias, a_log, h0, h0, ctx_out)
    y, xh, _, _ = ssm_bidir(xbc, dt, dt_bias, a_log, hcf, hcb, True)
    out = merge_branches(y, xh, z, gates, attn, ssm_d, ssm_norm_g, w_br_ssm, w_br_attn, w_out)
    if not ctx_out:
        return out, None
    qc, zc, gatesc = query_side(pc)
    attnc = context_attention(qc.reshape(b, C, KV_HEADS, Q_PER_KV, HEAD_DIM), kc, vc, sink)
    outc = merge_branches(yc, xhc, zc, gatesc, attnc, ssm_d, ssm_norm_g, w_br_ssm, w_br_attn, w_out)
    return out, outc


def peer_ffn(u, wq, keys1, keys2, pu, pv):
    b, L, _ = u.shape
    q = (u @ wq).reshape(b, L, PEER_HEADS, 2, KEY_HALF)
    s1 = jnp.einsum('blhd,kd->blhk', q[..., 0, :], keys1).astype(jnp.float32)
    s2 = jnp.einsum('blhd,kd->blhk', q[..., 1, :], keys2).astype(jnp.float32)
    v1, i1 = lax.top_k(s1, PEER_TOPK)
    v2, i2 = lax.top_k(s2, PEER_TOPK)
    cand_s = (v1[..., :, None] + v2[..., None, :]).reshape(b, L, PEER_HEADS, PEER_TOPK * PEER_TOPK)
    cand_i = (i1[..., :, None] * N_KEYS + i2[..., None, :]).reshape(b, L, PEER_HEADS, PEER_TOPK * PEER_TOPK)
    top_s, pos = lax.top_k(cand_s, PEER_TOPK)
    expert_idx = jnp.take_along_axis(cand_i, pos, axis=-1)
    gate = jax.nn.softmax(top_s, axis=-1).astype(u.dtype)
    n_blk = (b * L) // TOKEN_BLOCK
    xs = u.reshape(n_blk, TOKEN_BLOCK, D_MODEL)
    idx = expert_idx.reshape(n_blk, TOKEN_BLOCK, PEER_HEADS * PEER_TOPK)
    gs = gate.reshape(n_blk, TOKEN_BLOCK, PEER_HEADS * PEER_TOPK)

    def block(args):
        xb, ib, gb = args
        act = jax.nn.gelu(jnp.einsum('td,ted->te', xb, pu[ib]), approximate=False)
        return jnp.einsum('te,ted->td', gb * act, pv[ib])

    return lax.map(block, (xs, idx, gs)).reshape(b, L, D_MODEL)


def setup_inputs(seed: int = 0) -> dict:
    key = jax.random.key(seed)
    ks = jax.random.split(key, 26)

    def nrm(k, shape, scale):
        return jax.random.normal(k, shape, jnp.float32) * scale

    dt0 = jnp.exp(jax.random.uniform(ks[10], (DEPTH, 2, SSM_HEADS), jnp.float32,
                                     minval=math.log(1e-3), maxval=math.log(1e-1)))
    return {
        "x": nrm(ks[0], (BATCH, SEQ, D_MODEL), 1.0),
        "c": nrm(ks[1], (BATCH, D_MODEL), 1.0),
        "ctx": nrm(ks[2], (BATCH, CTX_LEN, D_MODEL), 1.0),
        "c_ctx": nrm(ks[3], (D_MODEL,), 1.0),
        "ada_w": nrm(ks[4], (DEPTH, D_MODEL, 6 * D_MODEL), D_MODEL ** -0.5),
        "ada_b": nrm(ks[5], (DEPTH, 6 * D_MODEL), 0.02),
        "norm1_g": 1.0 + nrm(ks[6], (DEPTH, D_MODEL), 0.02),
        "w_in": nrm(ks[7], (DEPTH, D_MODEL, TOTAL_COLS), D_MODEL ** -0.5),
        "conv_w": nrm(ks[8], (DEPTH, D_CONV, XBC_W), D_CONV ** -0.5),
        "conv_b": nrm(ks[9], (DEPTH, XBC_W), 0.02),
        "dt_bias": dt0 + jnp.log(-jnp.expm1(-dt0)),
        "a_log": jnp.log(jax.random.uniform(ks[11], (DEPTH, 2, SSM_HEADS), jnp.float32, minval=1.0, maxval=16.0)),
        "ssm_d": 1.0 + nrm(ks[12], (DEPTH, SSM_HEADS), 0.02),
        "ssm_norm_g": 1.0 + nrm(ks[13], (DEPTH, D_INNER), 0.02),
        "attn_sink": nrm(ks[14], (DEPTH, ATTN_HEADS), 0.5),
        "w_branch_ssm": nrm(ks[15], (DEPTH, D_INNER, D_MODEL), D_INNER ** -0.5),
        "w_branch_attn": nrm(ks[16], (DEPTH, Q_W, D_MODEL), Q_W ** -0.5),
        "w_out": nrm(ks[17], (DEPTH, D_MODEL, D_MODEL), D_MODEL ** -0.5),
        "norm2_g": 1.0 + nrm(ks[18], (DEPTH, D_MODEL), 0.02),
        "peer_wq": nrm(ks[19], (DEPTH, D_MODEL, PEER_HEADS * PEER_KEY_DIM), D_MODEL ** -0.5),
        "peer_keys1": nrm(ks[20], (DEPTH, N_KEYS, KEY_HALF), KEY_HALF ** -0.5),
        "peer_keys2": nrm(ks[21], (DEPTH, N_KEYS, KEY_HALF), KEY_HALF ** -0.5),
        "peer_u": nrm(ks[22], (DEPTH, N_EXPERTS, D_MODEL), D_MODEL ** -0.5),
        "peer_v": nrm(ks[23], (DEPTH, N_EXPERTS, D_MODEL), PEER_HEADS ** -0.5),
        "final_norm_g": 1.0 + nrm(ks[24], (D_MODEL,), 0.02),
    }


def reference(x, c, ctx, c_ctx, ada_w, ada_b, norm1_g, w_in, conv_w, conv_b, dt_bias, a_log, ssm_d,
              ssm_norm_g, attn_sink, w_branch_ssm, w_branch_attn, w_out, norm2_g, peer_wq, peer_keys1,
              peer_keys2, peer_u, peer_v, final_norm_g):
    b, S, _ = x.shape
    ROWS = S // GRID_W
    cos, sin = axial_rope_tables(ROWS)
    silu_c = jax.nn.silu(c)
    silu_cc = jax.nn.silu(c_ctx)
    h, hc = x, ctx
    for l in range(DEPTH):
        ctx_out = l < DEPTH - 1
        mod = (silu_c @ ada_w[l] + ada_b[l]).reshape(b, 6, 1, D_MODEL)
        modc = (silu_cc @ ada_w[l] + ada_b[l]).reshape(6, D_MODEL)
        u = modulate(rms_norm(h, norm1_g[l]), mod[:, 0], mod[:, 1])
        uc = modulate(rms_norm(hc, norm1_g[l]), modc[0], modc[1])
        mix, mixc = mixing_sublayer(u, uc, w_in[l], conv_w[l], conv_b[l], dt_bias[l], a_log[l], ssm_d[l],
                                    ssm_norm_g[l], attn_sink[l], w_branch_ssm[l], w_branch_attn[l],
                                    w_out[l], cos, sin, ctx_out)
        h = h + mod[:, 2] * mix
        u = modulate(rms_norm(h, norm2_g[l]), mod[:, 3], mod[:, 4])
        h = h + mod[:, 5] * peer_ffn(u, peer_wq[l], peer_keys1[l], peer_keys2[l], peer_u[l], peer_v[l])
        if ctx_out:
            hc = hc + modc[2] * mixc
            uc = modulate(rms_norm(hc, norm2_g[l]), modc[3], modc[4])
            hc = hc + modc[5] * peer_ffn(uc, peer_wq[l], peer_keys1[l], peer_keys2[l], peer_u[l], peer_v[l])
    return rms_norm(h, final_norm_g)
```

```python
import functools
import math

import jax
import jax.numpy as jnp
from jax import lax
from jax.experimental import pallas as pl
from jax.experimental.pallas import tpu as pltpu

F32 = jnp.float32
BF16 = jnp.bfloat16
HIGHEST = lax.Precision.HIGHEST

D_MODEL = 2048
GRID_W = 64
EPS = 1e-6
NEG_INF = -1e30
D_INNER = 2 * D_MODEL
SSM_HEADDIM = 64
SSM_HEADS = D_INNER // SSM_HEADDIM
SSM_GROUPS = 8
HEADS_PER_GROUP = SSM_HEADS // SSM_GROUPS
GROUP_W = HEADS_PER_GROUP * SSM_HEADDIM
D_STATE = 128
D_CONV = 5
CHUNK = 128
HEAD_DIM = 128
ATTN_HEADS = D_MODEL // HEAD_DIM
KV_HEADS = 4
Q_PER_KV = ATTN_HEADS // KV_HEADS
ATTN_BLOCK = 128
ROPE_BASE = 10000.0
ROPE_PAIRS = HEAD_DIM // 4
PEER_HEADS = 8
N_KEYS = 128
N_EXPERTS = N_KEYS * N_KEYS
KEY_HALF = 128
PEER_TOPK = 16
K_W = KV_HEADS * HEAD_DIM
GN_W = SSM_GROUPS * D_STATE
XBC_W = D_INNER + 2 * GN_W
DT_W = 2 * SSM_HEADS
Q_W = ATTN_HEADS * HEAD_DIM
Z_W = D_INNER
GATE_W = 2 * D_MODEL
CTX_COLS = 2 * K_W + XBC_W + DT_W

VMEM_LIMIT_BYTES = 52 * 1024 * 1024


def _params(*sem):
    return pltpu.CompilerParams(dimension_semantics=sem, vmem_limit_bytes=VMEM_LIMIT_BYTES)


def _ada_kernel(c_ref, w_ref, b_ref, o_ref):
    c = c_ref[...]
    s = c * jax.nn.sigmoid(c)
    o_ref[...] = jnp.dot(s.astype(BF16), w_ref[...].astype(BF16),
                         preferred_element_type=F32) + b_ref[...]


def _ada(cc, w, b):
    rows, d = cc.shape
    n = w.shape[1]
    tn = 1024
    return pl.pallas_call(
        _ada_kernel,
        grid=(n // tn,),
        in_specs=[pl.BlockSpec((rows, d), lambda j: (0, 0)),
                  pl.BlockSpec((d, tn), lambda j: (0, j)),
                  pl.BlockSpec((1, tn), lambda j: (0, j))],
        out_specs=pl.BlockSpec((rows, tn), lambda j: (0, j)),
        out_shape=jax.ShapeDtypeStruct((rows, n), F32),
        compiler_params=_params("arbitrary"),
        name="ada_mod",
    )(cc, w, b)


def _norm_mod_kernel(x_ref, g_ref, sc_ref, sh_ref, o_ref, *t_ref):
    x = x_ref[0]
    ms = jnp.mean(x * x, axis=-1, keepdims=True)
    xn = x * lax.rsqrt(ms + EPS)
    u = (xn * g_ref[...]) * (1.0 + sc_ref[0]) + sh_ref[0]
    o_ref[0] = u.astype(BF16)
    if t_ref:
        t_ref[0][...] = u.T.astype(BF16)


def _norm_mod(x, g, scale, shift, transposed=False):
    b, s, d = x.shape
    ts = min(256, s)
    nt = s // ts
    out_shape = [jax.ShapeDtypeStruct((b, s, d), BF16)]
    out_specs = [pl.BlockSpec((1, ts, d), lambda i, j: (i, j, 0))]
    if transposed:
        out_shape.append(jax.ShapeDtypeStruct((d, b * s), BF16))
        out_specs.append(pl.BlockSpec((d, ts), lambda i, j: (0, i * nt + j)))
    return pl.pallas_call(
        _norm_mod_kernel,
        grid=(b, nt),
        in_specs=[pl.BlockSpec((1, ts, d), lambda i, j: (i, j, 0)),
                  pl.BlockSpec((1, d), lambda i, j: (0, 0)),
                  pl.BlockSpec((1, 1, d), lambda i, j: (i, 0, 0)),
                  pl.BlockSpec((1, 1, d), lambda i, j: (i, 0, 0))],
        out_specs=out_specs,
        out_shape=out_shape,
        compiler_params=_params("arbitrary", "arbitrary"),
        name="norm_mod",
    )(x, g.reshape(1, d), scale, shift)


def _mm_kernel(a_ref, w_ref, o_ref):
    o_ref[...] = jnp.dot(a_ref[...], w_ref[...], preferred_element_type=F32).astype(o_ref.dtype)


def _mm_rope_kernel(a_ref, w_ref, cos_ref, sa_ref, sb_ref, o_ref):
    acc = jnp.dot(a_ref[...], w_ref[...], preferred_element_type=F32)
    cos, sa, sb = cos_ref[...], sa_ref[...], sb_ref[...]
    for j in range(acc.shape[1] // HEAD_DIM):
        t = acc[:, j * HEAD_DIM:(j + 1) * HEAD_DIM]
        r = t * cos + pltpu.roll(t, HEAD_DIM - ROPE_PAIRS, 1) * sa + pltpu.roll(t, ROPE_PAIRS, 1) * sb
        o_ref[:, j * HEAD_DIM:(j + 1) * HEAD_DIM] = r.astype(o_ref.dtype)


def _mm(a, w, out_dtype, rope=None, seq=None):
    m, k = a.shape
    n = w.shape[1]
    tm = min(1024, m if seq is None else seq)
    tn = min(512, n)
    in_specs = [pl.BlockSpec((tm, k), lambda i, j: (i, 0)),
                pl.BlockSpec((k, tn), lambda i, j: (0, j))]
    args = [a, w]
    kern = _mm_kernel
    if rope is not None:
        per = seq // tm
        in_specs += [pl.BlockSpec((tm, HEAD_DIM), lambda i, j: (i % per, 0))] * 3
        args += list(rope)
        kern = _mm_rope_kernel
    return pl.pallas_call(
        kern,
        grid=(m // tm, n // tn),
        in_specs=in_specs,
        out_specs=pl.BlockSpec((tm, tn), lambda i, j: (i, j)),
        out_shape=jax.ShapeDtypeStruct((m, n), out_dtype),
        compiler_params=_params("arbitrary", "arbitrary"),
        name="in_proj_rope" if rope is not None else "in_proj",
    )(*args)


def _attn_kernel(sink_ref, q_ref, kp_ref, kc_ref, kn_ref, vp_ref, vc_ref, vn_ref, kx_ref, vx_ref, o_ref):
    h = pl.program_id(1)
    n = pl.program_id(2)
    nb = pl.num_programs(2)
    blk = ATTN_BLOCK
    q = q_ref[0]
    qs = jnp.concatenate([q[:, r * HEAD_DIM:(r + 1) * HEAD_DIM] for r in range(Q_PER_KV)], axis=0)
    kcat = jnp.concatenate([kp_ref[0], kc_ref[0], kn_ref[0], kx_ref[0]], axis=0)
    vcat = jnp.concatenate([vp_ref[0], vc_ref[0], vn_ref[0], vx_ref[0]], axis=0)
    nk = kcat.shape[0]
    nq = Q_PER_KV * blk
    s = lax.dot_general(qs, kcat, (((1,), (1,)), ((), ())), preferred_element_type=F32) * (HEAD_DIM ** -0.5)
    qi = lax.broadcasted_iota(jnp.int32, (nq, nk), 0) & (blk - 1)
    kj = lax.broadcasted_iota(jnp.int32, (nq, nk), 1)
    off_prev = jnp.where(n > 0, 0, 4 * blk)
    off_next = jnp.where(n < nb - 1, 0, 4 * blk)
    bad_prev = (kj < blk) & (kj < qi + off_prev)
    bad_next = (kj >= 2 * blk) & (kj < 3 * blk) & (kj - 2 * blk > qi - off_next)
    s = jnp.where(bad_prev | bad_next, NEG_INF, s)
    rr = lax.broadcasted_iota(jnp.int32, (nq, 1), 0) // blk
    sk = jnp.zeros((nq, 1), F32)
    for r in range(Q_PER_KV):
        sk = jnp.where(rr == r, sink_ref[h * Q_PER_KV + r], sk)
    mx = jnp.maximum(jnp.max(s, axis=1, keepdims=True), sk)
    e = jnp.exp(s - mx)
    den = jnp.sum(e, axis=1, keepdims=True) + jnp.exp(sk - mx)
    p = (e / den).astype(BF16)
    o = jnp.dot(p, vcat, preferred_element_type=F32)
    for r in range(Q_PER_KV):
        o_ref[0, :, r * HEAD_DIM:(r + 1) * HEAD_DIM] = o[r * blk:(r + 1) * blk, :].astype(o_ref.dtype)


def _attention(q, k, v, kc, vc, sink):
    b, s, _ = q.shape
    c = kc.shape[1]
    nb = s // ATTN_BLOCK
    qw = Q_PER_KV * HEAD_DIM
    kv_spec = lambda f: pl.BlockSpec((1, ATTN_BLOCK, HEAD_DIM), f)
    prev = lambda bi, h, n: (bi, jnp.maximum(n - 1, 0), h)
    cur = lambda bi, h, n: (bi, n, h)
    nxt = lambda bi, h, n: (bi, jnp.minimum(n + 1, nb - 1), h)
    ctx_spec = pl.BlockSpec((1, c, HEAD_DIM), lambda bi, h, n: (bi, 0, h))
    return pl.pallas_call(
        _attn_kernel,
        grid=(b, KV_HEADS, nb),
        in_specs=[pl.BlockSpec(memory_space=pltpu.SMEM),
                  pl.BlockSpec((1, ATTN_BLOCK, qw), lambda bi, h, n: (bi, n, h)),
                  kv_spec(prev), kv_spec(cur), kv_spec(nxt),
                  kv_spec(prev), kv_spec(cur), kv_spec(nxt),
                  ctx_spec, ctx_spec],
        out_specs=pl.BlockSpec((1, ATTN_BLOCK, qw), lambda bi, h, n: (bi, n, h)),
        out_shape=jax.ShapeDtypeStruct((b, s, Q_W), BF16),
        compiler_params=_params("arbitrary", "arbitrary", "arbitrary"),
        name="window_attn",
    )(sink, q, k, k, k, v, v, v, kc, vc)


def _conv_kernel(x_ref, w_ref, b_ref, o_ref):
    x = x_ref[0]
    l = x.shape[0]
    w = w_ref[...]
    t = lax.broadcasted_iota(jnp.int32, (l, 1), 0)
    acc = x * w[D_CONV // 2:D_CONV // 2 + 1, :]
    for j in range(D_CONV):
        off = j - D_CONV // 2
        if off == 0:
            continue
        xs = pltpu.roll(x, (-off) % l, 0)
        ok = (t + off >= 0) & (t + off < l)
        acc = acc + jnp.where(ok, xs, 0.0) * w[j:j + 1, :]
    acc = acc + b_ref[...]
    o_ref[0] = (acc * jax.nn.sigmoid(acc)).astype(o_ref.dtype)


def _conv_silu(xbc, w, bias, col0, width):
    b, l, _ = xbc.shape
    tc = 256
    c0 = col0 // tc
    return pl.pallas_call(
        _conv_kernel,
        grid=(b, width // tc),
        in_specs=[pl.BlockSpec((1, l, tc), lambda i, j: (i, 0, c0 + j)),
                  pl.BlockSpec((D_CONV, tc), lambda i, j: (0, c0 + j)),
                  pl.BlockSpec((1, tc), lambda i, j: (0, c0 + j))],
        out_specs=pl.BlockSpec((1, l, tc), lambda i, j: (i, 0, j)),
        out_shape=jax.ShapeDtypeStruct((b, l, width), F32),
        compiler_params=_params("arbitrary", "arbitrary"),
        name="dwconv_silu",
    )(xbc, w, bias)


def _dt_kernel(dt_ref, bias_ref, alog_ref, dto_ref, acs_ref):
    raw = dt_ref[0] + bias_ref[...]
    dt = jnp.maximum(raw, 0.0) + jnp.log1p(jnp.exp(-jnp.abs(raw)))
    a = dt * (-jnp.exp(alog_ref[...]))
    li = lax.broadcasted_iota(jnp.int32, (CHUNK, CHUNK), 0)
    si = lax.broadcasted_iota(jnp.int32, (CHUNK, CHUNK), 1)
    lower = jnp.where(li >= si, 1.0, 0.0).astype(F32)
    upper = jnp.where(li <= si, 1.0, 0.0).astype(F32)
    pre = jnp.dot(lower, a, preferred_element_type=F32, precision=HIGHEST)
    suf = jnp.dot(upper, a, preferred_element_type=F32, precision=HIGHEST)
    lane = lax.broadcasted_iota(jnp.int32, (1, DT_W), 1)
    dto_ref[0] = dt
    acs_ref[0] = jnp.where(lane < SSM_HEADS, pre, suf)


def _dt_prep(dt_raw, dt_bias, a_log):
    b, l, _ = dt_raw.shape
    spec = pl.BlockSpec((1, CHUNK, DT_W), lambda i, c: (i, c, 0))
    vec = pl.BlockSpec((1, DT_W), lambda i, c: (0, 0))
    return pl.pallas_call(
        _dt_kernel,
        grid=(b, l // CHUNK),
        in_specs=[spec, vec, vec],
        out_specs=[spec, spec],
        out_shape=[jax.ShapeDtypeStruct((b, l, DT_W), F32)] * 2,
        compiler_params=_params("arbitrary", "arbitrary"),
        name="dt_prep",
    )(dt_raw, dt_bias.reshape(1, DT_W), a_log.reshape(1, DT_W))


def _group_layouts(t, direction):
    b, l, _ = t.shape
    t = t[..., direction * SSM_HEADS:(direction + 1) * SSM_HEADS].reshape(b, l, SSM_GROUPS, HEADS_PER_GROUP)
    return jnp.transpose(t, (0, 2, 1, 3)), jnp.transpose(t, (0, 2, 3, 1))


def _pair_expand(v, as_row=False):
    lo = lax.broadcasted_iota(jnp.int32, (1, 2 * SSM_HEADDIM), 1) < SSM_HEADDIM
    parts = []
    for i in range(HEADS_PER_GROUP // 2):
        parts.append(jnp.where(lo, v[:, 2 * i:2 * i + 1], v[:, 2 * i + 1:2 * i + 2]))
    return jnp.concatenate(parts, axis=1)


def _ssd_kernel(*refs, reverse, emit_y, add_prev):
    it = iter(refs)
    x_ref, b_ref = next(it), next(it)
    c_ref = next(it) if emit_y else None
    dtc_ref, acsc_ref = next(it), next(it)
    acsr_ref = next(it) if emit_y else None
    h0_ref = next(it)
    if add_prev:
        yp_ref, dexp_ref = next(it), next(it)
    y_ref = next(it) if emit_y else None
    hf_ref = next(it)
    hs_ref = next(it)

    c = pl.program_id(2)

    @pl.when(c == 0)
    def _():
        hs_ref[...] = h0_ref[0, 0]

    xg = x_ref[0]
    bt = b_ref[0].T.astype(BF16)
    dtc = dtc_ref[0, 0]
    acs_c = acsc_ref[0, 0]
    last = acs_c[0:1, :] if reverse else acs_c[CHUNK - 1:CHUNK, :]
    wx = xg * _pair_expand(dtc)
    wxe = wx * _pair_expand(jnp.exp(last - acs_c))
    hs = hs_ref[...]
    states = jnp.dot(bt, wxe.astype(BF16), preferred_element_type=F32)
    hs_new = _pair_expand(jnp.exp(last)) * hs + states
    hs_ref[...] = hs_new
    hf_ref[0, 0] = hs_new

    if emit_y:
        cb16 = c_ref[0].astype(BF16)
        y_off = jnp.dot(cb16, hs.astype(BF16), preferred_element_type=F32) * _pair_expand(jnp.exp(acs_c))
        cbm = lax.dot_general(cb16, b_ref[0].astype(BF16), (((1,), (1,)), ((), ())),
                              preferred_element_type=F32)
        acs_r = acsr_ref[0, 0]
        li = lax.broadcasted_iota(jnp.int32, (CHUNK, CHUNK), 0)
        si = lax.broadcasted_iota(jnp.int32, (CHUNK, CHUNK), 1)
        causal = (li <= si) if reverse else (li >= si)
        lo = lax.broadcasted_iota(jnp.int32, (1, 2 * SSM_HEADDIM), 1) < SSM_HEADDIM
        wx16 = wx.astype(BF16)
        parts = []
        for i in range(HEADS_PER_GROUP // 2):
            ys = []
            for j in (2 * i, 2 * i + 1):
                seg = acs_c[:, j:j + 1] - acs_r[j:j + 1, :]
                dec = jnp.exp(jnp.where(causal, seg, -jnp.inf))
                m = (cbm * dec).astype(BF16)
                ys.append(jnp.dot(m, wx16[:, i * 128:(i + 1) * 128], preferred_element_type=F32))
            parts.append(jnp.where(lo, ys[0], ys[1]))
        y = jnp.concatenate(parts, axis=1) + y_off
        if add_prev:
            y = (yp_ref[0] + y) + dexp_ref[...] * xg
        y_ref[0] = y


def _ssd_pass(x, bm, cm, dt, acs, h0, direction, emit_y, y_prev=None, d_exp=None):
    b, l, _ = x.shape
    nc = l // CHUNK
    reverse = direction == 1
    add_prev = y_prev is not None
    dtc, _ = _group_layouts(dt, direction)
    acsc, acsr = _group_layouts(acs, direction)
    cidx = (lambda c: nc - 1 - c) if reverse else (lambda c: c)
    xspec = pl.BlockSpec((1, CHUNK, GROUP_W), lambda i, g, c: (i, cidx(c), g))
    bspec = pl.BlockSpec((1, CHUNK, D_STATE), lambda i, g, c: (i, cidx(c), g))
    colspec = pl.BlockSpec((1, 1, CHUNK, HEADS_PER_GROUP), lambda i, g, c: (i, g, cidx(c), 0))
    rowspec = pl.BlockSpec((1, 1, HEADS_PER_GROUP, CHUNK), lambda i, g, c: (i, g, 0, cidx(c)))
    hspec = pl.BlockSpec((1, 1, D_STATE, GROUP_W), lambda i, g, c: (i, g, 0, 0))
    in_specs, args = [xspec, bspec], [x, bm]
    if emit_y:
        in_specs.append(bspec)
        args.append(cm)
    in_specs += [colspec, colspec]
    args += [dtc, acsc]
    if emit_y:
        in_specs.append(rowspec)
        args.append(acsr)
    in_specs.append(hspec)
    args.append(h0)
    if add_prev:
        in_specs += [xspec, pl.BlockSpec((1, GROUP_W), lambda i, g, c: (0, g))]
        args += [y_prev, d_exp]
    out_specs, out_shape = [], []
    if emit_y:
        out_specs.append(xspec)
        out_shape.append(jax.ShapeDtypeStruct((b, l, D_INNER), F32))
    out_specs.append(hspec)
    out_shape.append(jax.ShapeDtypeStruct((b, SSM_GROUPS, D_STATE, GROUP_W), F32))
    res = pl.pallas_call(
        functools.partial(_ssd_kernel, reverse=reverse, emit_y=emit_y, add_prev=add_prev),
        grid=(b, SSM_GROUPS, nc),
        in_specs=in_specs,
        out_specs=out_specs,
        out_shape=out_shape,
        scratch_shapes=[pltpu.VMEM((D_STATE, GROUP_W), F32)],
        compiler_params=_params("arbitrary", "arbitrary", "arbitrary"),
        name="ssd_scan",
    )(*args)
    return (res[0], res[1]) if emit_y else (None, res[0])


def _gated_norm_kernel(y_ref, z_ref, g_ref, o_ref):
    z = z_ref[...]
    t = y_ref[...] * (z * jax.nn.sigmoid(z))
    ms = jnp.mean(t * t, axis=-1, keepdims=True)
    o_ref[...] = ((t * lax.rsqrt(ms + EPS)) * g_ref[...]).astype(o_ref.dtype)


def _gated_norm(y, z, g):
    m, d = y.shape
    tr = min(256, m)
    spec = pl.BlockSpec((tr, d), lambda i: (i, 0))
    return pl.pallas_call(
        _gated_norm_kernel,
        grid=(m // tr,),
        in_specs=[spec, spec, pl.BlockSpec((1, d), lambda i: (0, 0))],
        out_specs=spec,
        out_shape=jax.ShapeDtypeStruct((m, d), BF16),
        compiler_params=_params("arbitrary"),
        name="gated_norm",
    )(y, z, g.reshape(1, d))


def _merge_kernel(yn_ref, at_ref, ws_ref, wa_ref, g1_ref, g2_ref, o_ref):
    t1 = jnp.dot(yn_ref[...], ws_ref[...], preferred_element_type=F32)
    t2 = jnp.dot(at_ref[...], wa_ref[...], preferred_element_type=F32)
    o_ref[...] = (jax.nn.sigmoid(g1_ref[...]) * t1 + jax.nn.sigmoid(g2_ref[...]) * t2).astype(o_ref.dtype)


def _merge(yn, attn, w_ssm, w_attn, gates):
    m = yn.shape[0]
    tm = min(512, m)
    tn = 512
    nj = D_MODEL // tn
    return pl.pallas_call(
        _merge_kernel,
        grid=(m // tm, nj),
        in_specs=[pl.BlockSpec((tm, D_INNER), lambda i, j: (i, 0)),
                  pl.BlockSpec((tm, Q_W), lambda i, j: (i, 0)),
                  pl.BlockSpec((D_INNER, tn), lambda i, j: (0, j)),
                  pl.BlockSpec((Q_W, tn), lambda i, j: (0, j)),
                  pl.BlockSpec((tm, tn), lambda i, j: (i, j)),
                  pl.BlockSpec((tm, tn), lambda i, j: (i, nj + j))],
        out_specs=pl.BlockSpec((tm, tn), lambda i, j: (i, j)),
        out_shape=jax.ShapeDtypeStruct((m, D_MODEL), BF16),
        compiler_params=_params("arbitrary", "arbitrary"),
        name="branch_merge",
    )(yn, attn, w_ssm, w_attn, gates, gates)


def _out_proj_kernel(a_ref, w_ref, x_ref, g_ref, o_ref):
    mix = jnp.dot(a_ref[0], w_ref[...], preferred_element_type=F32)
    o_ref[0] = x_ref[0] + g_ref[0] * mix


def _out_proj(merged, w_out, x, gate):
    b, s, d = x.shape
    tm = min(512, s)
    tn = 512
    return pl.pallas_call(
        _out_proj_kernel,
        grid=(b, s // tm, d // tn),
        in_specs=[pl.BlockSpec((1, tm, d), lambda i, r, j: (i, r, 0)),
                  pl.BlockSpec((d, tn), lambda i, r, j: (0, j)),
                  pl.BlockSpec((1, tm, tn), lambda i, r, j: (i, r, j)),
                  pl.BlockSpec((1, 1, tn), lambda i, r, j: (i, 0, j))],
        out_specs=pl.BlockSpec((1, tm, tn), lambda i, r, j: (i, r, j)),
        out_shape=jax.ShapeDtypeStruct((b, s, d), F32),
        compiler_params=_params("arbitrary", "arbitrary", "arbitrary"),
        name="out_proj",
    )(merged, w_out, x, gate)


PEER_TT = 256
_META_ROWS = 8


def _topk_rows(s, n_rounds):
    rows = s.shape[0]
    rid = lax.broadcasted_iota(jnp.int32, s.shape, 0).astype(F32)
    rank = jnp.full(s.shape, float(n_rounds), F32)
    vals = []
    idx = None
    for r in range(n_rounds):
        m = jnp.max(s, axis=0, keepdims=True)
        idx = jnp.min(jnp.where(s == m, rid, float(rows)), axis=0, keepdims=True)
        hit = rid == idx
        rank = jnp.where(hit, float(r), rank)
        s = jnp.where(hit, -jnp.inf, s)
        vals.append(m)
    return jnp.concatenate(vals, axis=0), rank, idx


def _peer_topk_kernel(u_ref, wq_ref, k1_ref, k2_ref, s1_ref, ra_ref, s2_ref, rb_ref, meta_ref, q_scr):
    q_scr[...] = jnp.dot(u_ref[...], wq_ref[...], preferred_element_type=F32)
    tt = u_ref.shape[0]

    def head(h, carry):
        base = pl.multiple_of(h * 2 * KEY_HALF, 2 * KEY_HALF)
        q1 = q_scr[:, pl.ds(base, KEY_HALF)].astype(BF16)
        q2 = q_scr[:, pl.ds(pl.multiple_of(base + KEY_HALF, KEY_HALF), KEY_HALF)].astype(BF16)
        nt = (((1,), (1,)), ((), ()))
        sc1 = lax.dot_general(k1_ref[...], q1, nt, preferred_element_type=F32)
        sc2 = lax.dot_general(k2_ref[...], q2, nt, preferred_element_type=F32)
        v1, r1, _ = _topk_rows(sc1, PEER_TOPK)
        v2, r2, _ = _topk_rows(sc2, PEER_TOPK)
        cand = jnp.concatenate([v1[a:a + 1, :] + v2 for a in range(PEER_TOPK)], axis=0)
        top, _, pos = _topk_rows(cand, PEER_TOPK)
        mx = top[0:1, :]
        z = jnp.sum(jnp.exp(top - mx), axis=0, keepdims=True)
        s1_ref[h] = jnp.where(r1 < PEER_TOPK, sc1, -jnp.inf)
        s2_ref[h] = jnp.where(r2 < PEER_TOPK, sc2, -jnp.inf)
        ra_ref[h] = r1
        rb_ref[h] = r2
        pos_a = jnp.floor(pos * (1.0 / PEER_TOPK))
        pos_b = pos - pos_a * PEER_TOPK
        meta_ref[h] = jnp.concatenate(
            [top[PEER_TOPK - 1:PEER_TOPK, :], pos_a, pos_b, v1[0:1, :], v2[0:1, :], 1.0 / z,
             jnp.zeros((_META_ROWS - 6, tt), F32)], axis=0)
        return carry

    lax.fori_loop(0, PEER_HEADS, head, 0)


def _peer_topk(u2, wq, keys1, keys2):
    m, d = u2.shape
    tt = min(PEER_TT, m)
    big = pl.BlockSpec((PEER_HEADS, N_KEYS, tt), lambda i: (0, 0, i))
    bshape = jax.ShapeDtypeStruct((PEER_HEADS, N_KEYS, m), F32)
    kspec = pl.BlockSpec((N_KEYS, KEY_HALF), lambda i: (0, 0))
    return pl.pallas_call(
        _peer_topk_kernel,
        grid=(m // tt,),
        in_specs=[pl.BlockSpec((tt, d), lambda i: (i, 0)),
                  pl.BlockSpec((d, PEER_HEADS * 2 * KEY_HALF), lambda i: (0, 0)),
                  kspec, kspec],
        out_specs=[big, big, big, big, pl.BlockSpec((PEER_HEADS, _META_ROWS, tt), lambda i: (0, 0, i))],
        out_shape=[bshape, bshape, bshape, bshape,
                   jax.ShapeDtypeStruct((PEER_HEADS, _META_ROWS, m), F32)],
        scratch_shapes=[pltpu.VMEM((tt, PEER_HEADS * 2 * KEY_HALF), F32)],
        compiler_params=_params("arbitrary"),
        name="peer_topk",
    )(u2, wq, keys1, keys2)


PEER_NI = 4


def _peer_dense_kernel(ut_ref, pu_ref, pvt_ref, s1_ref, ra_ref, s2_ref, rb_ref, meta_ref,
                       h_ref, g5_ref, gf_ref, o_ref, acc_ref, e2_ref):
    e = pl.program_id(1)
    ne = pl.num_programs(1)

    @pl.when(e == 0)
    def _():
        acc_ref[...] = jnp.zeros_like(acc_ref)
        for h in range(PEER_HEADS):
            e2_ref[h] = jnp.exp(s2_ref[h] - meta_ref[h, 4:5, :])

    ht = jnp.dot(pu_ref[...], ut_ref[...], preferred_element_type=F32)
    for ii in range(PEER_NI):
        i1 = e * PEER_NI + ii
        hb = ht[ii * N_KEYS:(ii + 1) * N_KEYS, :]
        act = 0.5 * hb * (1.0 + lax.erf(hb * np_sqrt_half))
        g = jnp.zeros_like(hb)
        for h in range(PEER_HEADS):
            s1c = s1_ref[h, pl.ds(i1, 1), :]
            rac = ra_ref[h, pl.ds(i1, 1), :]
            tau = meta_ref[h, 0:1, :]
            pa = meta_ref[h, 1:2, :]
            pb = meta_ref[h, 2:3, :]
            e1c = jnp.exp(s1c - meta_ref[h, 3:4, :]) * meta_ref[h, 5:6, :]
            qbc = jnp.where(rac < pa, float(PEER_TOPK), jnp.where(rac == pa, pb, -1.0))
            s = s1c + s2_ref[h]
            sel = (s > tau) | ((s == tau) & (rb_ref[h] <= qbc))
            g = g + jnp.where(sel, e1c * e2_ref[h], 0.0)
        ga = (g * act).astype(BF16)
        acc_ref[...] += jnp.dot(pvt_ref[:, ii * N_KEYS:(ii + 1) * N_KEYS], ga, preferred_element_type=F32)

    @pl.when(e == ne - 1)
    def _():
        hres = h_ref[0] + g5_ref[0] * acc_ref[...].T
        ms = jnp.mean(hres * hres, axis=-1, keepdims=True)
        o_ref[0] = (hres * lax.rsqrt(ms + EPS)) * gf_ref[...]


np_sqrt_half = float(math.sqrt(0.5))


def _peer_dense(u2t, pu, pvt, s1f, ra, s2f, rb, meta, h1, gate5, gfinal):
    b, s, d = h1.shape
    m = b * s
    tt = min(PEER_TT, s)
    per = s // tt
    te = PEER_NI * N_KEYS
    big = pl.BlockSpec((PEER_HEADS, N_KEYS, tt), lambda i, e: (0, 0, i))
    return pl.pallas_call(
        _peer_dense_kernel,
        grid=(m // tt, N_EXPERTS // te),
        in_specs=[pl.BlockSpec((d, tt), lambda i, e: (0, i)),
                  pl.BlockSpec((te, d), lambda i, e: (e, 0)),
                  pl.BlockSpec((d, te), lambda i, e: (0, e)),
                  big, big, big, big,
                  pl.BlockSpec((PEER_HEADS, _META_ROWS, tt), lambda i, e: (0, 0, i)),
                  pl.BlockSpec((1, tt, d), lambda i, e: (i // per, i % per, 0)),
                  pl.BlockSpec((1, 1, d), lambda i, e: (i // per, 0, 0)),
                  pl.BlockSpec((1, d), lambda i, e: (0, 0))],
        out_specs=pl.BlockSpec((1, tt, d), lambda i, e: (i // per, i % per, 0)),
        out_shape=jax.ShapeDtypeStruct((b, s, d), F32),
        scratch_shapes=[pltpu.VMEM((d, tt), F32), pltpu.VMEM((PEER_HEADS, N_KEYS, tt), F32)],
        compiler_params=_params("arbitrary", "arbitrary"),
        name="peer_dense",
    )(u2t, pu, pvt, s1f, ra, s2f, rb, meta, h1, gate5, gfinal.reshape(1, d))


def _rope_tables(rows):
    row = jnp.repeat(jnp.arange(rows), GRID_W).astype(F32)
    col = jnp.tile(jnp.arange(GRID_W), rows).astype(F32)
    freqs = ROPE_BASE ** (-jnp.arange(ROPE_PAIRS, dtype=F32) / ROPE_PAIRS)
    ar = row[:, None] * freqs
    ac = col[:, None] * freqs
    ang = jnp.concatenate([ar, ar, ac, ac], axis=-1)
    cos, sin = jnp.cos(ang), jnp.sin(ang)
    lane = jnp.arange(HEAD_DIM)
    first = (lane % (2 * ROPE_PAIRS)) < ROPE_PAIRS
    return cos, jnp.where(first, -sin, 0.0), jnp.where(first, 0.0, sin)


def kernel(x, c, ctx, c_ctx, ada_w, ada_b, norm1_g, w_in, conv_w, conv_b, dt_bias, a_log, ssm_d, ssm_norm_g,
           attn_sink, w_branch_ssm, w_branch_attn, w_out, norm2_g, peer_wq, peer_keys1, peer_keys2, peer_u,
           peer_v, final_norm_g):
    b, s, d = x.shape
    cl = ctx.shape[1]
    m = b * s
    l = 0

    cc = jnp.concatenate([c, c_ctx[None, :], jnp.zeros((8 - b - 1, d), F32)], axis=0)
    mod_all = _ada(cc, ada_w[l], ada_b[l][None, :])
    mod = mod_all[:b].reshape(b, 6, 1, d)
    modc = jnp.broadcast_to(mod_all[b].reshape(1, 6, 1, d), (b, 6, 1, d))

    (u,) = _norm_mod(x, norm1_g[l], mod[:, 1], mod[:, 0])
    (uc,) = _norm_mod(ctx, norm1_g[l], modc[:, 1], modc[:, 0])
    u = u.reshape(m, d)
    uc = uc.reshape(b * cl, d)

    wi = w_in[l]
    o = 0
    w_k = wi[:, o:o + K_W].astype(BF16); o += K_W
    w_v = wi[:, o:o + K_W].astype(BF16); o += K_W
    w_xbc = wi[:, o:o + XBC_W].astype(BF16); o += XBC_W
    w_dt = wi[:, o:o + DT_W].astype(BF16); o += DT_W
    w_q = wi[:, o:o + Q_W].astype(BF16); o += Q_W
    w_z = wi[:, o:o + Z_W].astype(BF16); o += Z_W
    w_g = wi[:, o:o + GATE_W].astype(BF16)

    rope = _rope_tables(s // GRID_W)
    q = _mm(u, w_q, BF16, rope=rope, seq=s).reshape(b, s, Q_W)
    k = _mm(u, w_k, BF16, rope=rope, seq=s).reshape(b, s, K_W)
    v = _mm(u, w_v, BF16).reshape(b, s, K_W)
    xbc = _mm(u, w_xbc, F32).reshape(b, s, XBC_W)
    dt_raw = _mm(u, w_dt, F32).reshape(b, s, DT_W)
    z = _mm(u, w_z, F32)
    gates = _mm(u, w_g, F32)
    kc = _mm(uc, w_k, BF16).reshape(b, cl, K_W)
    vc = _mm(uc, w_v, BF16).reshape(b, cl, K_W)
    xbcc = _mm(uc, w_xbc, F32).reshape(b, cl, XBC_W)
    dtc_raw = _mm(uc, w_dt, F32).reshape(b, cl, DT_W)

    attn = _attention(q, k, v, kc, vc, attn_sink[l]).reshape(m, Q_W)

    cw, cb = conv_w[l], conv_b[l][None, :]
    xs = _conv_silu(xbc, cw, cb, 0, D_INNER)
    bm = _conv_silu(xbc, cw, cb, D_INNER, GN_W)
    cm = _conv_silu(xbc, cw, cb, D_INNER + GN_W, GN_W)
    xsc = _conv_silu(xbcc, cw, cb, 0, D_INNER)
    bmc = _conv_silu(xbcc, cw, cb, D_INNER, GN_W)
    dt, acs = _dt_prep(dt_raw, dt_bias[l], a_log[l])
    dtc, acsc = _dt_prep(dtc_raw, dt_bias[l], a_log[l])
    hzero = jnp.zeros((b, SSM_GROUPS, D_STATE, GROUP_W), F32)
    _, hcf = _ssd_pass(xsc, bmc, None, dtc, acsc, hzero, 0, False)
    _, hcb = _ssd_pass(xsc, bmc, None, dtc, acsc, hzero, 1, False)
    y_f, _ = _ssd_pass(xs, bm, cm, dt, acs, hcf, 0, True)
    d_exp = jnp.repeat(ssm_d[l], SSM_HEADDIM)[None, :]
    y, _ = _ssd_pass(xs, bm, cm, dt, acs, hcb, 1, True, y_prev=y_f, d_exp=d_exp)

    yn = _gated_norm(y.reshape(m, D_INNER), z, ssm_norm_g[l])
    merged = _merge(yn, attn, w_branch_ssm[l].astype(BF16), w_branch_attn[l].astype(BF16), gates)
    h1 = _out_proj(merged.reshape(b, s, d), w_out[l].astype(BF16), x, mod[:, 2])

    u2, u2t = _norm_mod(h1, norm2_g[l], mod[:, 4], mod[:, 3], transposed=True)
    s1f, ra, s2f, rb, meta = _peer_topk(u2.reshape(m, d), peer_wq[l].astype(BF16),
                                        peer_keys1[l].astype(BF16), peer_keys2[l].astype(BF16))
    pu = peer_u[l].astype(BF16)
    pvt = peer_v[l].T.astype(BF16)
    return _peer_dense(u2t, pu, pvt, s1f, ra, s2f, rb, meta, h1, mod[:, 5], final_norm_g)
```

```python
import functools
import math

import jax
import jax.numpy as jnp
from jax import lax
from jax.experimental import pallas as pl
from jax.experimental.pallas import tpu as pltpu

F32 = jnp.float32
BF16 = jnp.bfloat16
HIGHEST = lax.Precision.HIGHEST

D_MODEL = 2048
GRID_W = 64
EPS = 1e-6
NEG_INF = -1e30
D_INNER = 2 * D_MODEL
SSM_HEADDIM = 64
SSM_HEADS = D_INNER // SSM_HEADDIM
SSM_GROUPS = 8
HEADS_PER_GROUP = SSM_HEADS // SSM_GROUPS
GROUP_W = HEADS_PER_GROUP * SSM_HEADDIM
D_STATE = 128
D_CONV = 5
CHUNK = 128
HEAD_DIM = 128
ATTN_HEADS = D_MODEL // HEAD_DIM
KV_HEADS = 4
Q_PER_KV = ATTN_HEADS // KV_HEADS
ATTN_BLOCK = 128
ROPE_BASE = 10000.0
ROPE_PAIRS = HEAD_DIM // 4
PEER_HEADS = 8
N_KEYS = 128
N_EXPERTS = N_KEYS * N_KEYS
KEY_HALF = 128
PEER_TOPK = 16
K_W = KV_HEADS * HEAD_DIM
GN_W = SSM_GROUPS * D_STATE
XBC_W = D_INNER + 2 * GN_W
DT_W = 2 * SSM_HEADS
Q_W = ATTN_HEADS * HEAD_DIM
Z_W = D_INNER
GATE_W = 2 * D_MODEL
CTX_COLS = 2 * K_W + XBC_W + DT_W

VMEM_LIMIT_BYTES = 52 * 1024 * 1024


def _params(*sem):
    return pltpu.CompilerParams(dimension_semantics=sem, vmem_limit_bytes=VMEM_LIMIT_BYTES)


def _ada_kernel(c_ref, w_ref, b_ref, o_ref):
    c = c_ref[...]
    s = c * jax.nn.sigmoid(c)
    o_ref[...] = jnp.dot(s.astype(BF16), w_ref[...].astype(BF16),
                         preferred_element_type=F32) + b_ref[...]


def _ada(cc, w, b):
    rows, d = cc.shape
    n = w.shape[1]
    tn = 1024
    return pl.pallas_call(
        _ada_kernel,
        grid=(n // tn,),
        in_specs=[pl.BlockSpec((rows, d), lambda j: (0, 0)),
                  pl.BlockSpec((d, tn), lambda j: (0, j)),
                  pl.BlockSpec((1, tn), lambda j: (0, j))],
        out_specs=pl.BlockSpec((rows, tn), lambda j: (0, j)),
        out_shape=jax.ShapeDtypeStruct((rows, n), F32),
        compiler_params=_params("arbitrary"),
        name="ada_mod",
    )(cc, w, b)


def _norm_mod_kernel(x_ref, g_ref, sc_ref, sh_ref, o_ref, *t_ref):
    x = x_ref[0]
    ms = jnp.mean(x * x, axis=-1, keepdims=True)
    xn = x * lax.rsqrt(ms + EPS)
    u = (xn * g_ref[...]) * (1.0 + sc_ref[0]) + sh_ref[0]
    o_ref[0] = u.astype(BF16)
    if t_ref:
        t_ref[0][...] = u.T.astype(BF16)


def _norm_mod(x, g, scale, shift, transposed=False):
    b, s, d = x.shape
    ts = min(256, s)
    nt = s // ts
    out_shape = [jax.ShapeDtypeStruct((b, s, d), BF16)]
    out_specs = [pl.BlockSpec((1, ts, d), lambda i, j: (i, j, 0))]
    if transposed:
        out_shape.append(jax.ShapeDtypeStruct((d, b * s), BF16))
        out_specs.append(pl.BlockSpec((d, ts), lambda i, j: (0, i * nt + j)))
    return pl.pallas_call(
        _norm_mod_kernel,
        grid=(b, nt),
        in_specs=[pl.BlockSpec((1, ts, d), lambda i, j: (i, j, 0)),
                  pl.BlockSpec((1, d), lambda i, j: (0, 0)),
                  pl.BlockSpec((1, 1, d), lambda i, j: (i, 0, 0)),
                  pl.BlockSpec((1, 1, d), lambda i, j: (i, 0, 0))],
        out_specs=out_specs,
        out_shape=out_shape,
        compiler_params=_params("arbitrary", "arbitrary"),
        name="norm_mod",
    )(x, g.reshape(1, d), scale, shift)


def _mm_kernel(a_ref, w_ref, o_ref):
    o_ref[...] = jnp.dot(a_ref[...], w_ref[...], preferred_element_type=F32).astype(o_ref.dtype)


def _mm_rope_kernel(a_ref, w_ref, cos_ref, sa_ref, sb_ref, o_ref):
    acc = jnp.dot(a_ref[...], w_ref[...], preferred_element_type=F32)
    cos, sa, sb = cos_ref[...], sa_ref[...], sb_ref[...]
    for j in range(acc.shape[1] // HEAD_DIM):
        t = acc[:, j * HEAD_DIM:(j + 1) * HEAD_DIM]
        r = t * cos + pltpu.roll(t, HEAD_DIM - ROPE_PAIRS, 1) * sa + pltpu.roll(t, ROPE_PAIRS, 1) * sb
        o_ref[:, j * HEAD_DIM:(j + 1) * HEAD_DIM] = r.astype(o_ref.dtype)


def _mm(a, w, out_dtype, rope=None, seq=None):
    m, k = a.shape
    n = w.shape[1]
    tm = min(1024, m if seq is None else seq)
    tn = min(512, n)
    in_specs = [pl.BlockSpec((tm, k), lambda i, j: (i, 0)),
                pl.BlockSpec((k, tn), lambda i, j: (0, j))]
    args = [a, w]
    kern = _mm_kernel
    if rope is not None:
        per = seq // tm
        in_specs += [pl.BlockSpec((tm, HEAD_DIM), lambda i, j: (i % per, 0))] * 3
        args += list(rope)
        kern = _mm_rope_kernel
    return pl.pallas_call(
        kern,
        grid=(m // tm, n // tn),
        in_specs=in_specs,
        out_specs=pl.BlockSpec((tm, tn), lambda i, j: (i, j)),
        out_shape=jax.ShapeDtypeStruct((m, n), out_dtype),
        compiler_params=_params("arbitrary", "arbitrary"),
        name="in_proj_rope" if rope is not None else "in_proj",
    )(*args)


def _attn_kernel(sink_ref, q_ref, kp_ref, kc_ref, kn_ref, vp_ref, vc_ref, vn_ref, kx_ref, vx_ref, o_ref):
    h = pl.program_id(1)
    n = pl.program_id(2)
    nb = pl.num_programs(2)
    blk = ATTN_BLOCK
    q = q_ref[0]
    qs = jnp.concatenate([q[:, r * HEAD_DIM:(r + 1) * HEAD_DIM] for r in range(Q_PER_KV)], axis=0)
    kcat = jnp.concatenate([kp_ref[0], kc_ref[0], kn_ref[0], kx_ref[0]], axis=0)
    vcat = jnp.concatenate([vp_ref[0], vc_ref[0], vn_ref[0], vx_ref[0]], axis=0)
    nk = kcat.shape[0]
    nq = Q_PER_KV * blk
    s = lax.dot_general(qs, kcat, (((1,), (1,)), ((), ())), preferred_element_type=F32) * (HEAD_DIM ** -0.5)
    qi = lax.broadcasted_iota(jnp.int32, (nq, nk), 0) & (blk - 1)
    kj = lax.broadcasted_iota(jnp.int32, (nq, nk), 1)
    off_prev = jnp.where(n > 0, 0, 4 * blk)
    off_next = jnp.where(n < nb - 1, 0, 4 * blk)
    bad_prev = (kj < blk) & (kj < qi + off_prev)
    bad_next = (kj >= 2 * blk) & (kj < 3 * blk) & (kj - 2 * blk > qi - off_next)
    s = jnp.where(bad_prev | bad_next, NEG_INF, s)
    rr = lax.broadcasted_iota(jnp.int32, (nq, 1), 0) // blk
    sk = jnp.zeros((nq, 1), F32)
    for r in range(Q_PER_KV):
        sk = jnp.where(rr == r, sink_ref[h * Q_PER_KV + r], sk)
    mx = jnp.maximum(jnp.max(s, axis=1, keepdims=True), sk)
    e = jnp.exp(s - mx)
    den = jnp.sum(e, axis=1, keepdims=True) + jnp.exp(sk - mx)
    p = (e / den).astype(BF16)
    o = jnp.dot(p, vcat, preferred_element_type=F32)
    for r in range(Q_PER_KV):
        o_ref[0, :, r * HEAD_DIM:(r + 1) * HEAD_DIM] = o[r * blk:(r + 1) * blk, :].astype(o_ref.dtype)


def _attention(q, k, v, kc, vc, sink):
    b, s, _ = q.shape
    c = kc.shape[1]
    nb = s // ATTN_BLOCK
    qw = Q_PER_KV * HEAD_DIM
    kv_spec = lambda f: pl.BlockSpec((1, ATTN_BLOCK, HEAD_DIM), f)
    prev = lambda bi, h, n: (bi, jnp.maximum(n - 1, 0), h)
    cur = lambda bi, h, n: (bi, n, h)
    nxt = lambda bi, h, n: (bi, jnp.minimum(n + 1, nb - 1), h)
    ctx_spec = pl.BlockSpec((1, c, HEAD_DIM), lambda bi, h, n: (bi, 0, h))
    return pl.pallas_call(
        _attn_kernel,
        grid=(b, KV_HEADS, nb),
        in_specs=[pl.BlockSpec(memory_space=pltpu.SMEM),
                  pl.BlockSpec((1, ATTN_BLOCK, qw), lambda bi, h, n: (bi, n, h)),
                  kv_spec(prev), kv_spec(cur), kv_spec(nxt),
                  kv_spec(prev), kv_spec(cur), kv_spec(nxt),
                  ctx_spec, ctx_spec],
        out_specs=pl.BlockSpec((1, ATTN_BLOCK, qw), lambda bi, h, n: (bi, n, h)),
        out_shape=jax.ShapeDtypeStruct((b, s, Q_W), BF16),
        compiler_params=_params("arbitrary", "arbitrary", "arbitrary"),
        name="window_attn",
    )(sink, q, k, k, k, v, v, v, kc, vc)


def _conv_kernel(x_ref, w_ref, b_ref, o_ref):
    x = x_ref[0]
    l = x.shape[0]
    w = w_ref[...]
    t = lax.broadcasted_iota(jnp.int32, (l, 1), 0)
    acc = x * w[D_CONV // 2:D_CONV // 2 + 1, :]
    for j in range(D_CONV):
        off = j - D_CONV // 2
        if off == 0:
            continue
        xs = pltpu.roll(x, (-off) % l, 0)
        ok = (t + off >= 0) & (t + off < l)
        acc = acc + jnp.where(ok, xs, 0.0) * w[j:j + 1, :]
    acc = acc + b_ref[...]
    o_ref[0] = (acc * jax.nn.sigmoid(acc)).astype(o_ref.dtype)


def _conv_silu(xbc, w, bias, col0, width):
    b, l, _ = xbc.shape
    tc = 256
    c0 = col0 // tc
    return pl.pallas_call(
        _conv_kernel,
        grid=(b, width // tc),
        in_specs=[pl.BlockSpec((1, l, tc), lambda i, j: (i, 0, c0 + j)),
                  pl.BlockSpec((D_CONV, tc), lambda i, j: (0, c0 + j)),
                  pl.BlockSpec((1, tc), lambda i, j: (0, c0 + j))],
        out_specs=pl.BlockSpec((1, l, tc), lambda i, j: (i, 0, j)),
        out_shape=jax.ShapeDtypeStruct((b, l, width), F32),
        compiler_params=_params("arbitrary", "arbitrary"),
        name="dwconv_silu",
    )(xbc, w, bias)


def _dt_kernel(dt_ref, bias_ref, alog_ref, dto_ref, acs_ref):
    raw = dt_ref[0] + bias_ref[...]
    dt = jnp.maximum(raw, 0.0) + jnp.log1p(jnp.exp(-jnp.abs(raw)))
    a = dt * (-jnp.exp(alog_ref[...]))
    li = lax.broadcasted_iota(jnp.int32, (CHUNK, CHUNK), 0)
    si = lax.broadcasted_iota(jnp.int32, (CHUNK, CHUNK), 1)
    lower = jnp.where(li >= si, 1.0, 0.0).astype(F32)
    upper = jnp.where(li <= si, 1.0, 0.0).astype(F32)
    pre = jnp.dot(lower, a, preferred_element_type=F32, precision=HIGHEST)
    suf = jnp.dot(upper, a, preferred_element_type=F32, precision=HIGHEST)
    lane = lax.broadcasted_iota(jnp.int32, (1, DT_W), 1)
    dto_ref[0] = dt
    acs_ref[0] = jnp.where(lane < SSM_HEADS, pre, suf)


def _dt_prep(dt_raw, dt_bias, a_log):
    b, l, _ = dt_raw.shape
    spec = pl.BlockSpec((1, CHUNK, DT_W), lambda i, c: (i, c, 0))
    vec = pl.BlockSpec((1, DT_W), lambda i, c: (0, 0))
    return pl.pallas_call(
        _dt_kernel,
        grid=(b, l // CHUNK),
        in_specs=[spec, vec, vec],
        out_specs=[spec, spec],
        out_shape=[jax.ShapeDtypeStruct((b, l, DT_W), F32)] * 2,
        compiler_params=_params("arbitrary", "arbitrary"),
        name="dt_prep",
    )(dt_raw, dt_bias.reshape(1, DT_W), a_log.reshape(1, DT_W))


def _group_layouts(t, direction):
    b, l, _ = t.shape
    t = t[..., direction * SSM_HEADS:(direction + 1) * SSM_HEADS].reshape(b, l, SSM_GROUPS, HEADS_PER_GROUP)
    return jnp.transpose(t, (0, 2, 1, 3)), jnp.transpose(t, (0, 2, 3, 1))


def _pair_expand(v, as_row=False):
    lo = lax.broadcasted_iota(jnp.int32, (1, 2 * SSM_HEADDIM), 1) < SSM_HEADDIM
    parts = []
    for i in range(HEADS_PER_GROUP // 2):
        parts.append(jnp.where(lo, v[:, 2 * i:2 * i + 1], v[:, 2 * i + 1:2 * i + 2]))
    return jnp.concatenate(parts, axis=1)


def _ssd_kernel(*refs, reverse, emit_y, add_prev):
    it = iter(refs)
    x_ref, b_ref = next(it), next(it)
    c_ref = next(it) if emit_y else None
    dtc_ref, acsc_ref = next(it), next(it)
    acsr_ref = next(it) if emit_y else None
    h0_ref = next(it)
    if add_prev:
        yp_ref, dexp_ref = next(it), next(it)
    y_ref = next(it) if emit_y else None
    hf_ref = next(it)
    hs_ref = next(it)

    c = pl.program_id(2)

    @pl.when(c == 0)
    def _():
        hs_ref[...] = h0_ref[0, 0]

    xg = x_ref[0]
    bt = b_ref[0].T.astype(BF16)
    dtc = dtc_ref[0, 0]
    acs_c = acsc_ref[0, 0]
    last = acs_c[0:1, :] if reverse else acs_c[CHUNK - 1:CHUNK, :]
    wx = xg * _pair_expand(dtc)
    wxe = wx * _pair_expand(jnp.exp(last - acs_c))
    hs = hs_ref[...]
    states = jnp.dot(bt, wxe.astype(BF16), preferred_element_type=F32)
    hs_new = _pair_expand(jnp.exp(last)) * hs + states
    hs_ref[...] = hs_new
    hf_ref[0, 0] = hs_new

    if emit_y:
        cb16 = c_ref[0].astype(BF16)
        y_off = jnp.dot(cb16, hs.astype(BF16), preferred_element_type=F32) * _pair_expand(jnp.exp(acs_c))
        cbm = lax.dot_general(cb16, b_ref[0].astype(BF16), (((1,), (1,)), ((), ())),
                              preferred_element_type=F32)
        acs_r = acsr_ref[0, 0]
        li = lax.broadcasted_iota(jnp.int32, (CHUNK, CHUNK), 0)
        si = lax.broadcasted_iota(jnp.int32, (CHUNK, CHUNK), 1)
        causal = (li <= si) if reverse else (li >= si)
        lo = lax.broadcasted_iota(jnp.int32, (1, 2 * SSM_HEADDIM), 1) < SSM_HEADDIM
        wx16 = wx.astype(BF16)
        parts = []
        for i in range(HEADS_PER_GROUP // 2):
            ys = []
            for j in (2 * i, 2 * i + 1):
                seg = acs_c[:, j:j + 1] - acs_r[j:j + 1, :]
                dec = jnp.exp(jnp.where(causal, seg, -jnp.inf))
                m = (cbm * dec).astype(BF16)
                ys.append(jnp.dot(m, wx16[:, i * 128:(i + 1) * 128], preferred_element_type=F32))
            parts.append(jnp.where(lo, ys[0], ys[1]))
        y = jnp.concatenate(parts, axis=1) + y_off
        if add_prev:
            y = (yp_ref[0] + y) + dexp_ref[...] * xg
        y_ref[0] = y


def _ssd_pass(x, bm, cm, dt, acs, h0, direction, emit_y, y_prev=None, d_exp=None):
    b, l, _ = x.shape
    nc = l // CHUNK
    reverse = direction == 1
    add_prev = y_prev is not None
    dtc, _ = _group_layouts(dt, direction)
    acsc, acsr = _group_layouts(acs, direction)
    cidx = (lambda c: nc - 1 - c) if reverse else (lambda c: c)
    xspec = pl.BlockSpec((1, CHUNK, GROUP_W), lambda i, g, c: (i, cidx(c), g))
    bspec = pl.BlockSpec((1, CHUNK, D_STATE), lambda i, g, c: (i, cidx(c), g))
    colspec = pl.BlockSpec((1, 1, CHUNK, HEADS_PER_GROUP), lambda i, g, c: (i, g, cidx(c), 0))
    rowspec = pl.BlockSpec((1, 1, HEADS_PER_GROUP, CHUNK), lambda i, g, c: (i, g, 0, cidx(c)))
    hspec = pl.BlockSpec((1, 1, D_STATE, GROUP_W), lambda i, g, c: (i, g, 0, 0))
    in_specs, args = [xspec, bspec], [x, bm]
    if emit_y:
        in_specs.append(bspec)
        args.append(cm)
    in_specs += [colspec, colspec]
    args += [dtc, acsc]
    if emit_y:
        in_specs.append(rowspec)
        args.append(acsr)
    in_specs.append(hspec)
    args.append(h0)
    if add_prev:
        in_specs += [xspec, pl.BlockSpec((1, GROUP_W), lambda i, g, c: (0, g))]
        args += [y_prev, d_exp]
    out_specs, out_shape = [], []
    if emit_y:
        out_specs.append(xspec)
        out_shape.append(jax.ShapeDtypeStruct((b, l, D_INNER), F32))
    out_specs.append(hspec)
    out_shape.append(jax.ShapeDtypeStruct((b, SSM_GROUPS, D_STATE, GROUP_W), F32))
    res = pl.pallas_call(
        functools.partial(_ssd_kernel, reverse=reverse, emit_y=emit_y, add_prev=add_prev),
        grid=(b, SSM_GROUPS, nc),
        in_specs=in_specs,
        out_specs=out_specs,
        out_shape=out_shape,
        scratch_shapes=[pltpu.VMEM((D_STATE, GROUP_W), F32)],
        compiler_params=_params("arbitrary", "arbitrary", "arbitrary"),
        name="ssd_scan",
    )(*args)
    return (res[0], res[1]) if emit_y else (None, res[0])


def _gated_norm_kernel(y_ref, z_ref, g_ref, o_ref):
    z = z_ref[...]
    t = y_ref[...] * (z * jax.nn.sigmoid(z))
    ms = jnp.mean(t * t, axis=-1, keepdims=True)
    o_ref[...] = ((t * lax.rsqrt(ms + EPS)) * g_ref[...]).astype(o_ref.dtype)


def _gated_norm(y, z, g):
    m, d = y.shape
    tr = min(256, m)
    spec = pl.BlockSpec((tr, d), lambda i: (i, 0))
    return pl.pallas_call(
        _gated_norm_kernel,
        grid=(m // tr,),
        in_specs=[spec, spec, pl.BlockSpec((1, d), lambda i: (0, 0))],
        out_specs=spec,
        out_shape=jax.ShapeDtypeStruct((m, d), BF16),
        compiler_params=_params("arbitrary"),
        name="gated_norm",
    )(y, z, g.reshape(1, d))


def _merge_kernel(yn_ref, at_ref, ws_ref, wa_ref, g1_ref, g2_ref, o_ref):
    t1 = jnp.dot(yn_ref[...], ws_ref[...], preferred_element_type=F32)
    t2 = jnp.dot(at_ref[...], wa_ref[...], preferred_element_type=F32)
    o_ref[...] = (jax.nn.sigmoid(g1_ref[...]) * t1 + jax.nn.sigmoid(g2_ref[...]) * t2).astype(o_ref.dtype)


def _merge(yn, attn, w_ssm, w_attn, gates):
    m = yn.shape[0]
    tm = min(512, m)
    tn = 512
    nj = D_MODEL // tn
    return pl.pallas_call(
        _merge_kernel,
        grid=(m // tm, nj),
        in_specs=[pl.BlockSpec((tm, D_INNER), lambda i, j: (i, 0)),
                  pl.BlockSpec((tm, Q_W), lambda i, j: (i, 0)),
                  pl.BlockSpec((D_INNER, tn), lambda i, j: (0, j)),
                  pl.BlockSpec((Q_W, tn), lambda i, j: (0, j)),
                  pl.BlockSpec((tm, tn), lambda i, j: (i, j)),
                  pl.BlockSpec((tm, tn), lambda i, j: (i, nj + j))],
        out_specs=pl.BlockSpec((tm, tn), lambda i, j: (i, j)),
        out_shape=jax.ShapeDtypeStruct((m, D_MODEL), BF16),
        compiler_params=_params("arbitrary", "arbitrary"),
        name="branch_merge",
    )(yn, attn, w_ssm, w_attn, gates, gates)


def _out_proj_kernel(a_ref, w_ref, x_ref, g_ref, o_ref):
    mix = jnp.dot(a_ref[0], w_ref[...], preferred_element_type=F32)
    o_ref[0] = x_ref[0] + g_ref[0] * mix


def _out_proj(merged, w_out, x, gate):
    b, s, d = x.shape
    tm = min(512, s)
    tn = 512
    return pl.pallas_call(
        _out_proj_kernel,
        grid=(b, s // tm, d // tn),
        in_specs=[pl.BlockSpec((1, tm, d), lambda i, r, j: (i, r, 0)),
                  pl.BlockSpec((d, tn), lambda i, r, j: (0, j)),
                  pl.BlockSpec((1, tm, tn), lambda i, r, j: (i, r, j)),
                  pl.BlockSpec((1, 1, tn), lambda i, r, j: (i, 0, j))],
        out_specs=pl.BlockSpec((1, tm, tn), lambda i, r, j: (i, r, j)),
        out_shape=jax.ShapeDtypeStruct((b, s, d), F32),
        compiler_params=_params("arbitrary", "arbitrary", "arbitrary"),
        name="out_proj",
    )(merged, w_out, x, gate)


PEER_TT = 256


def _topk_rows(s, n_rounds):
    rows = s.shape[0]
    rid = lax.broadcasted_iota(jnp.int32, s.shape, 0).astype(F32)
    rank = jnp.full(s.shape, float(n_rounds), F32)
    vals = []
    for r in range(n_rounds):
        m = jnp.max(s, axis=0, keepdims=True)
        idx = jnp.min(jnp.where(s == m, rid, float(rows)), axis=0, keepdims=True)
        hit = rid == idx
        rank = jnp.where(hit, float(r), rank)
        s = jnp.where(hit, -jnp.inf, s)
        vals.append(m)
    return jnp.concatenate(vals, axis=0), rank


def _staircase_counts(v1, v2):
    k = v1.shape[0]
    aid = lax.broadcasted_iota(jnp.int32, v1.shape, 0).astype(F32)
    n = jnp.zeros_like(v1)
    front = v1 + v2[0:1, :]
    picked = []
    for _ in range(k):
        m = jnp.max(front, axis=0, keepdims=True)
        idx = jnp.min(jnp.where(front == m, aid, float(k)), axis=0, keepdims=True)
        hit = aid == idx
        n = jnp.where(hit, n + 1.0, n)
        nxt = jnp.full_like(v1, -jnp.inf)
        for b in range(1, k):
            nxt = jnp.where(n == float(b), v2[b:b + 1, :], nxt)
        front = jnp.where(hit, v1 + nxt, front)
        picked.append(m)
    return n, jnp.concatenate(picked, axis=0)


def _peer_topk_kernel(u_ref, wq_ref, k1_ref, k2_ref, e1_ref, nc_ref, e2_ref, rb_ref, q_scr):
    q_scr[...] = jnp.dot(u_ref[...], wq_ref[...], preferred_element_type=F32)

    def head(h, carry):
        base = pl.multiple_of(h * 2 * KEY_HALF, 2 * KEY_HALF)
        q1 = q_scr[:, pl.ds(base, KEY_HALF)].astype(BF16)
        q2 = q_scr[:, pl.ds(pl.multiple_of(base + KEY_HALF, KEY_HALF), KEY_HALF)].astype(BF16)
        nt = (((1,), (1,)), ((), ()))
        sc1 = lax.dot_general(k1_ref[...], q1, nt, preferred_element_type=F32)
        sc2 = lax.dot_general(k2_ref[...], q2, nt, preferred_element_type=F32)
        v1, r1 = _topk_rows(sc1, PEER_TOPK)
        v2, r2 = _topk_rows(sc2, PEER_TOPK)
        n, top = _staircase_counts(v1, v2)
        z = jnp.sum(jnp.exp(top - top[0:1, :]), axis=0, keepdims=True)
        nc = jnp.zeros_like(sc1)
        for a in range(PEER_TOPK):
            nc = jnp.where(r1 == float(a), n[a:a + 1, :], nc)
        e1_ref[h] = jnp.where(r1 < float(PEER_TOPK), jnp.exp(sc1 - v1[0:1, :]) / z, 0.0)
        nc_ref[h] = nc
        e2_ref[h] = jnp.exp(sc2 - v2[0:1, :])
        rb_ref[h] = r2
        return carry

    lax.fori_loop(0, PEER_HEADS, head, 0)


def _peer_topk(u2, wq, keys1, keys2):
    m, d = u2.shape
    tt = min(PEER_TT, m)
    big = pl.BlockSpec((PEER_HEADS, N_KEYS, tt), lambda i: (0, 0, i))
    bshape = jax.ShapeDtypeStruct((PEER_HEADS, N_KEYS, m), F32)
    kspec = pl.BlockSpec((N_KEYS, KEY_HALF), lambda i: (0, 0))
    return pl.pallas_call(
        _peer_topk_kernel,
        grid=(m // tt,),
        in_specs=[pl.BlockSpec((tt, d), lambda i: (i, 0)),
                  pl.BlockSpec((d, PEER_HEADS * 2 * KEY_HALF), lambda i: (0, 0)),
                  kspec, kspec],
        out_specs=[big, big, big, big],
        out_shape=[bshape, bshape, bshape, bshape],
        scratch_shapes=[pltpu.VMEM((tt, PEER_HEADS * 2 * KEY_HALF), F32)],
        compiler_params=_params("arbitrary"),
        name="peer_topk",
    )(u2, wq, keys1, keys2)


PEER_NI = 8
SQRT_HALF = float(math.sqrt(0.5))


def _peer_stage(e, ut_ref, pu_ref, pvt_ref, e1_ref, nc_ref, e2_ref, rb_ref, acc_ref, ht_w, ht_r):
    ht_w[...] = jnp.dot(pu_ref[...], ut_ref[...], preferred_element_type=F32)
    gas = []
    for ii in range(PEER_NI):
        i1 = jnp.maximum(e - 1, 0) * PEER_NI + ii
        hb = ht_r[ii * N_KEYS:(ii + 1) * N_KEYS, :]
        act = 0.5 * hb * (1.0 + lax.erf(hb * SQRT_HALF))
        g = jnp.zeros_like(hb)
        for h in range(PEER_HEADS):
            e1c = e1_ref[h, pl.ds(i1, 1), :]
            ncc = nc_ref[h, pl.ds(i1, 1), :]
            g = g + jnp.where(rb_ref[h] < ncc, e1c * e2_ref[h], 0.0)
        gas.append((g * act).astype(BF16))
    acc_ref[...] += jnp.dot(pvt_ref[...], jnp.concatenate(gas, axis=0), preferred_element_type=F32)


def _peer_dense_kernel(ut_ref, pu_ref, pvt_ref, e1_ref, nc_ref, e2_ref, rb_ref,
                       h_ref, g5_ref, gf_ref, o_ref, acc_ref, ht0_ref, ht1_ref):
    e = pl.program_id(1)
    ne = pl.num_programs(1)

    @pl.when(e == 0)
    def _():
        acc_ref[...] = jnp.zeros_like(acc_ref)
        ht1_ref[...] = jnp.zeros_like(ht1_ref)

    stage = functools.partial(_peer_stage, e, ut_ref, pu_ref, pvt_ref, e1_ref, nc_ref, e2_ref, rb_ref, acc_ref)

    @pl.when(e % 2 == 0)
    def _():
        stage(ht0_ref, ht1_ref)

    @pl.when(e % 2 == 1)
    def _():
        stage(ht1_ref, ht0_ref)

    @pl.when(e == ne - 1)
    def _():
        hres = h_ref[0] + g5_ref[0] * acc_ref[...].T
        ms = jnp.mean(hres * hres, axis=-1, keepdims=True)
        o_ref[0] = (hres * lax.rsqrt(ms + EPS)) * gf_ref[...]


def _peer_dense(u2t, pu, pvt, e1, nc, e2, rb, h1, gate5, gfinal):
    b, s, d = h1.shape
    m = b * s
    tt = min(PEER_TT, s)
    per = s // tt
    te = PEER_NI * N_KEYS
    ne = N_EXPERTS // te
    big = pl.BlockSpec((PEER_HEADS, N_KEYS, tt), lambda i, e: (0, 0, i))
    return pl.pallas_call(
        _peer_dense_kernel,
        grid=(m // tt, ne + 1),
        in_specs=[pl.BlockSpec((d, tt), lambda i, e: (0, i)),
                  pl.BlockSpec((te, d), lambda i, e: (jnp.minimum(e, ne - 1), 0)),
                  pl.BlockSpec((d, te), lambda i, e: (0, jnp.maximum(e - 1, 0))),
                  big, big, big, big,
                  pl.BlockSpec((1, tt, d), lambda i, e: (i // per, i % per, 0)),
                  pl.BlockSpec((1, 1, d), lambda i, e: (i // per, 0, 0)),
                  pl.BlockSpec((1, d), lambda i, e: (0, 0))],
        out_specs=pl.BlockSpec((1, tt, d), lambda i, e: (i // per, i % per, 0)),
        out_shape=jax.ShapeDtypeStruct((b, s, d), F32),
        scratch_shapes=[pltpu.VMEM((d, tt), F32), pltpu.VMEM((te, tt), F32), pltpu.VMEM((te, tt), F32)],
        compiler_params=_params("arbitrary", "arbitrary"),
        name="peer_dense",
    )(u2t, pu, pvt, e1, nc, e2, rb, h1, gate5, gfinal.reshape(1, d))


def _rope_tables(rows):
    row = jnp.repeat(jnp.arange(rows), GRID_W).astype(F32)
    col = jnp.tile(jnp.arange(GRID_W), rows).astype(F32)
    freqs = ROPE_BASE ** (-jnp.arange(ROPE_PAIRS, dtype=F32) / ROPE_PAIRS)
    ar = row[:, None] * freqs
    ac = col[:, None] * freqs
    ang = jnp.concatenate([ar, ar, ac, ac], axis=-1)
    cos, sin = jnp.cos(ang), jnp.sin(ang)
    lane = jnp.arange(HEAD_DIM)
    first = (lane % (2 * ROPE_PAIRS)) < ROPE_PAIRS
    return cos, jnp.where(first, -sin, 0.0), jnp.where(first, 0.0, sin)


def kernel(x, c, ctx, c_ctx, ada_w, ada_b, norm1_g, w_in, conv_w, conv_b, dt_bias, a_log, ssm_d, ssm_norm_g,
           attn_sink, w_branch_ssm, w_branch_attn, w_out, norm2_g, peer_wq, peer_keys1, peer_keys2, peer_u,
           peer_v, final_norm_g):
    b, s, d = x.shape
    cl = ctx.shape[1]
    m = b * s
    l = 0

    cc = jnp.concatenate([c, c_ctx[None, :], jnp.zeros((8 - b - 1, d), F32)], axis=0)
    mod_all = _ada(cc, ada_w[l], ada_b[l][None, :])
    mod = mod_all[:b].reshape(b, 6, 1, d)
    modc = jnp.broadcast_to(mod_all[b].reshape(1, 6, 1, d), (b, 6, 1, d))

    (u,) = _norm_mod(x, norm1_g[l], mod[:, 1], mod[:, 0])
    (uc,) = _norm_mod(ctx, norm1_g[l], modc[:, 1], modc[:, 0])
    u = u.reshape(m, d)
    uc = uc.reshape(b * cl, d)

    wi = w_in[l]
    o = 0
    w_k = wi[:, o:o + K_W].astype(BF16); o += K_W
    w_v = wi[:, o:o + K_W].astype(BF16); o += K_W
    w_xbc = wi[:, o:o + XBC_W].astype(BF16); o += XBC_W
    w_dt = wi[:, o:o + DT_W].astype(BF16); o += DT_W
    w_q = wi[:, o:o + Q_W].astype(BF16); o += Q_W
    w_z = wi[:, o:o + Z_W].astype(BF16); o += Z_W
    w_g = wi[:, o:o + GATE_W].astype(BF16)

    rope = _rope_tables(s // GRID_W)
    q = _mm(u, w_q, BF16, rope=rope, seq=s).reshape(b, s, Q_W)
    k = _mm(u, w_k, BF16, rope=rope, seq=s).reshape(b, s, K_W)
    v = _mm(u, w_v, BF16).reshape(b, s, K_W)
    xbc = _mm(u, w_xbc, F32).reshape(b, s, XBC_W)
    dt_raw = _mm(u, w_dt, F32).reshape(b, s, DT_W)
    z = _mm(u, w_z, F32)
    gates = _mm(u, w_g, F32)
    kc = _mm(uc, w_k, BF16).reshape(b, cl, K_W)
    vc = _mm(uc, w_v, BF16).reshape(b, cl, K_W)
    xbcc = _mm(uc, w_xbc, F32).reshape(b, cl, XBC_W)
    dtc_raw = _mm(uc, w_dt, F32).reshape(b, cl, DT_W)

    attn = _attention(q, k, v, kc, vc, attn_sink[l]).reshape(m, Q_W)

    cw, cb = conv_w[l], conv_b[l][None, :]
    xs = _conv_silu(xbc, cw, cb, 0, D_INNER)
    bm = _conv_silu(xbc, cw, cb, D_INNER, GN_W)
    cm = _conv_silu(xbc, cw, cb, D_INNER + GN_W, GN_W)
    xsc = _conv_silu(xbcc, cw, cb, 0, D_INNER)
    bmc = _conv_silu(xbcc, cw, cb, D_INNER, GN_W)
    dt, acs = _dt_prep(dt_raw, dt_bias[l], a_log[l])
    dtc, acsc = _dt_prep(dtc_raw, dt_bias[l], a_log[l])
    hzero = jnp.zeros((b, SSM_GROUPS, D_STATE, GROUP_W), F32)
    _, hcf = _ssd_pass(xsc, bmc, None, dtc, acsc, hzero, 0, False)
    _, hcb = _ssd_pass(xsc, bmc, None, dtc, acsc, hzero, 1, False)
    y_f, _ = _ssd_pass(xs, bm, cm, dt, acs, hcf, 0, True)
    d_exp = jnp.repeat(ssm_d[l], SSM_HEADDIM)[None, :]
    y, _ = _ssd_pass(xs, bm, cm, dt, acs, hcb, 1, True, y_prev=y_f, d_exp=d_exp)

    yn = _gated_norm(y.reshape(m, D_INNER), z, ssm_norm_g[l])
    merged = _merge(yn, attn, w_branch_ssm[l].astype(BF16), w_branch_attn[l].astype(BF16), gates)
    h1 = _out_proj(merged.reshape(b, s, d), w_out[l].astype(BF16), x, mod[:, 2])

    u2, u2t = _norm_mod(h1, norm2_g[l], mod[:, 4], mod[:, 3], transposed=True)
    e1, nc, e2, rb = _peer_topk(u2.reshape(m, d), peer_wq[l].astype(BF16),
                                peer_keys1[l].astype(BF16), peer_keys2[l].astype(BF16))
    pu = peer_u[l].astype(BF16)
    pvt = peer_v[l].T.astype(BF16)
    return _peer_dense(u2t, pu, pvt, e1, nc, e2, rb, h1, mod[:, 5], final_norm_g)
```

```python
import functools
import math

import jax
import jax.numpy as jnp
from jax import lax
from jax.experimental import pallas as pl
from jax.experimental.pallas import tpu as pltpu

F32 = jnp.float32
BF16 = jnp.bfloat16
HIGHEST = lax.Precision.HIGHEST

D_MODEL = 2048
GRID_W = 64
EPS = 1e-6
NEG_INF = -1e30
D_INNER = 2 * D_MODEL
SSM_HEADDIM = 64
SSM_HEADS = D_INNER // SSM_HEADDIM
SSM_GROUPS = 8
HEADS_PER_GROUP = SSM_HEADS // SSM_GROUPS
GROUP_W = HEADS_PER_GROUP * SSM_HEADDIM
D_STATE = 128
D_CONV = 5
CHUNK = 128
HEAD_DIM = 128
ATTN_HEADS = D_MODEL // HEAD_DIM
KV_HEADS = 4
Q_PER_KV = ATTN_HEADS // KV_HEADS
ATTN_BLOCK = 128
ROPE_BASE = 10000.0
ROPE_PAIRS = HEAD_DIM // 4
PEER_HEADS = 8
N_KEYS = 128
N_EXPERTS = N_KEYS * N_KEYS
KEY_HALF = 128
PEER_TOPK = 16
K_W = KV_HEADS * HEAD_DIM
GN_W = SSM_GROUPS * D_STATE
XBC_W = D_INNER + 2 * GN_W
DT_W = 2 * SSM_HEADS
Q_W = ATTN_HEADS * HEAD_DIM
Z_W = D_INNER
GATE_W = 2 * D_MODEL
CTX_COLS = 2 * K_W + XBC_W + DT_W

VMEM_LIMIT_BYTES = 52 * 1024 * 1024
VMEM_LIMIT_PEER_BYTES = 58 * 1024 * 1024


def _params(*sem, vmem=VMEM_LIMIT_BYTES):
    return pltpu.CompilerParams(dimension_semantics=sem, vmem_limit_bytes=vmem)


def _ada_kernel(c_ref, w_ref, b_ref, o_ref):
    c = c_ref[...]
    s = c * jax.nn.sigmoid(c)
    o_ref[...] = jnp.dot(s.astype(BF16), w_ref[...].astype(BF16),
                         preferred_element_type=F32) + b_ref[...]


def _ada(cc, w, b):
    rows, d = cc.shape
    n = w.shape[1]
    tn = 1024
    return pl.pallas_call(
        _ada_kernel,
        grid=(n // tn,),
        in_specs=[pl.BlockSpec((rows, d), lambda j: (0, 0)),
                  pl.BlockSpec((d, tn), lambda j: (0, j)),
                  pl.BlockSpec((1, tn), lambda j: (0, j))],
        out_specs=pl.BlockSpec((rows, tn), lambda j: (0, j)),
        out_shape=jax.ShapeDtypeStruct((rows, n), F32),
        compiler_params=_params("arbitrary"),
        name="ada_mod",
    )(cc, w, b)


def _norm_mod_kernel(x_ref, g_ref, sc_ref, sh_ref, o_ref, *t_ref):
    x = x_ref[0]
    ms = jnp.mean(x * x, axis=-1, keepdims=True)
    xn = x * lax.rsqrt(ms + EPS)
    u = (xn * g_ref[...]) * (1.0 + sc_ref[0]) + sh_ref[0]
    o_ref[0] = u.astype(BF16)
    if t_ref:
        t_ref[0][...] = u.T.astype(BF16)


def _norm_mod(x, g, scale, shift, transposed=False):
    b, s, d = x.shape
    ts = min(256, s)
    nt = s // ts
    out_shape = [jax.ShapeDtypeStruct((b, s, d), BF16)]
    out_specs = [pl.BlockSpec((1, ts, d), lambda i, j: (i, j, 0))]
    if transposed:
        out_shape.append(jax.ShapeDtypeStruct((d, b * s), BF16))
        out_specs.append(pl.BlockSpec((d, ts), lambda i, j: (0, i * nt + j)))
    return pl.pallas_call(
        _norm_mod_kernel,
        grid=(b, nt),
        in_specs=[pl.BlockSpec((1, ts, d), lambda i, j: (i, j, 0)),
                  pl.BlockSpec((1, d), lambda i, j: (0, 0)),
                  pl.BlockSpec((1, 1, d), lambda i, j: (i, 0, 0)),
                  pl.BlockSpec((1, 1, d), lambda i, j: (i, 0, 0))],
        out_specs=out_specs,
        out_shape=out_shape,
        compiler_params=_params("arbitrary", "arbitrary"),
        name="norm_mod",
    )(x, g.reshape(1, d), scale, shift)


def _mm_kernel(a_ref, w_ref, o_ref):
    o_ref[...] = jnp.dot(a_ref[...], w_ref[...], preferred_element_type=F32).astype(o_ref.dtype)


def _mm_rope_kernel(a_ref, w_ref, cos_ref, sa_ref, sb_ref, o_ref):
    acc = jnp.dot(a_ref[...], w_ref[...], preferred_element_type=F32)
    cos, sa, sb = cos_ref[...], sa_ref[...], sb_ref[...]
    for j in range(acc.shape[1] // HEAD_DIM):
        t = acc[:, j * HEAD_DIM:(j + 1) * HEAD_DIM]
        r = t * cos + pltpu.roll(t, HEAD_DIM - ROPE_PAIRS, 1) * sa + pltpu.roll(t, ROPE_PAIRS, 1) * sb
        o_ref[:, j * HEAD_DIM:(j + 1) * HEAD_DIM] = r.astype(o_ref.dtype)


def _mm(a, w, out_dtype, rope=None, seq=None):
    m, k = a.shape
    n = w.shape[1]
    tm = min(1024, m if seq is None else seq)
    tn = min(512, n)
    in_specs = [pl.BlockSpec((tm, k), lambda i, j: (i, 0)),
                pl.BlockSpec((k, tn), lambda i, j: (0, j))]
    args = [a, w]
    kern = _mm_kernel
    if rope is not None:
        per = seq // tm
        in_specs += [pl.BlockSpec((tm, HEAD_DIM), lambda i, j: (i % per, 0))] * 3
        args += list(rope)
        kern = _mm_rope_kernel
    return pl.pallas_call(
        kern,
        grid=(m // tm, n // tn),
        in_specs=in_specs,
        out_specs=pl.BlockSpec((tm, tn), lambda i, j: (i, j)),
        out_shape=jax.ShapeDtypeStruct((m, n), out_dtype),
        compiler_params=_params("arbitrary", "arbitrary"),
        name="in_proj_rope" if rope is not None else "in_proj",
    )(*args)


ATTN_HEADS_PER_STEP = 2


def _attn_kernel(sink_ref, q_ref, kp_ref, kc_ref, kn_ref, vp_ref, vc_ref, vn_ref, kx_ref, vx_ref, o_ref):
    hg = pl.program_id(1)
    n = pl.program_id(2)
    nb = pl.num_programs(2)
    blk = ATTN_BLOCK
    nq = Q_PER_KV * blk
    nk = 3 * blk + kx_ref.shape[1]
    qi = lax.broadcasted_iota(jnp.int32, (nq, nk), 0) & (blk - 1)
    kj = lax.broadcasted_iota(jnp.int32, (nq, nk), 1)
    off_prev = jnp.where(n > 0, 0, 4 * blk)
    off_next = jnp.where(n < nb - 1, 0, 4 * blk)
    bad_prev = (kj < blk) & (kj < qi + off_prev)
    bad_next = (kj >= 2 * blk) & (kj < 3 * blk) & (kj - 2 * blk > qi - off_next)
    bad = bad_prev | bad_next
    rr = lax.broadcasted_iota(jnp.int32, (nq, 1), 0) // blk
    heads = range(ATTN_HEADS_PER_STEP)

    def head_cols(ref, i):
        return ref[0, :, i * HEAD_DIM:(i + 1) * HEAD_DIM]

    scores = []
    for i in heads:
        q = q_ref[0, :, i * Q_PER_KV * HEAD_DIM:(i + 1) * Q_PER_KV * HEAD_DIM]
        qs = jnp.concatenate([q[:, r * HEAD_DIM:(r + 1) * HEAD_DIM] for r in range(Q_PER_KV)], axis=0)
        kcat = jnp.concatenate([head_cols(kp_ref, i), head_cols(kc_ref, i), head_cols(kn_ref, i),
                                head_cols(kx_ref, i)], axis=0)
        scores.append(lax.dot_general(qs, kcat, (((1,), (1,)), ((), ())), preferred_element_type=F32))
    probs = []
    for i in heads:
        s = jnp.where(bad, NEG_INF, scores[i] * (HEAD_DIM ** -0.5))
        sk = jnp.zeros((nq, 1), F32)
        for r in range(Q_PER_KV):
            sk = jnp.where(rr == r, sink_ref[(hg * ATTN_HEADS_PER_STEP + i) * Q_PER_KV + r], sk)
        mx = jnp.maximum(jnp.max(s, axis=1, keepdims=True), sk)
        e = jnp.exp(s - mx)
        den = jnp.sum(e, axis=1, keepdims=True) + jnp.exp(sk - mx)
        probs.append((e / den).astype(BF16))
    for i in heads:
        vcat = jnp.concatenate([head_cols(vp_ref, i), head_cols(vc_ref, i), head_cols(vn_ref, i),
                                head_cols(vx_ref, i)], axis=0)
        o = jnp.dot(probs[i], vcat, preferred_element_type=F32)
        for r in range(Q_PER_KV):
            c0 = (i * Q_PER_KV + r) * HEAD_DIM
            o_ref[0, :, c0:c0 + HEAD_DIM] = o[r * blk:(r + 1) * blk, :].astype(o_ref.dtype)


def _attention(q, k, v, kc, vc, sink):
    b, s, _ = q.shape
    c = kc.shape[1]
    nb = s // ATTN_BLOCK
    qw = ATTN_HEADS_PER_STEP * Q_PER_KV * HEAD_DIM
    kw = ATTN_HEADS_PER_STEP * HEAD_DIM
    kv_spec = lambda f: pl.BlockSpec((1, ATTN_BLOCK, kw), f)
    prev = lambda bi, h, n: (bi, jnp.maximum(n - 1, 0), h)
    cur = lambda bi, h, n: (bi, n, h)
    nxt = lambda bi, h, n: (bi, jnp.minimum(n + 1, nb - 1), h)
    ctx_spec = pl.BlockSpec((1, c, kw), lambda bi, h, n: (bi, 0, h))
    return pl.pallas_call(
        _attn_kernel,
        grid=(b, KV_HEADS // ATTN_HEADS_PER_STEP, nb),
        in_specs=[pl.BlockSpec(memory_space=pltpu.SMEM),
                  pl.BlockSpec((1, ATTN_BLOCK, qw), lambda bi, h, n: (bi, n, h)),
                  kv_spec(prev), kv_spec(cur), kv_spec(nxt),
                  kv_spec(prev), kv_spec(cur), kv_spec(nxt),
                  ctx_spec, ctx_spec],
        out_specs=pl.BlockSpec((1, ATTN_BLOCK, qw), lambda bi, h, n: (bi, n, h)),
        out_shape=jax.ShapeDtypeStruct((b, s, Q_W), BF16),
        compiler_params=_params("arbitrary", "arbitrary", "arbitrary"),
        name="window_attn",
    )(sink, q, k, k, k, v, v, v, kc, vc)


def _conv_kernel(x_ref, w_ref, b_ref, o_ref):
    x = x_ref[0]
    l = x.shape[0]
    w = w_ref[...]
    t = lax.broadcasted_iota(jnp.int32, (l, 1), 0)
    acc = x * w[D_CONV // 2:D_CONV // 2 + 1, :]
    for j in range(D_CONV):
        off = j - D_CONV // 2
        if off == 0:
            continue
        xs = pltpu.roll(x, (-off) % l, 0)
        ok = (t + off >= 0) & (t + off < l)
        acc = acc + jnp.where(ok, xs, 0.0) * w[j:j + 1, :]
    acc = acc + b_ref[...]
    o_ref[0] = (acc * jax.nn.sigmoid(acc)).astype(o_ref.dtype)


def _conv_silu(xbc, w, bias, col0, width):
    b, l, _ = xbc.shape
    tc = 256
    c0 = col0 // tc
    return pl.pallas_call(
        _conv_kernel,
        grid=(b, width // tc),
        in_specs=[pl.BlockSpec((1, l, tc), lambda i, j: (i, 0, c0 + j)),
                  pl.BlockSpec((D_CONV, tc), lambda i, j: (0, c0 + j)),
                  pl.BlockSpec((1, tc), lambda i, j: (0, c0 + j))],
        out_specs=pl.BlockSpec((1, l, tc), lambda i, j: (i, 0, j)),
        out_shape=jax.ShapeDtypeStruct((b, l, width), F32),
        compiler_params=_params("arbitrary", "arbitrary"),
        name="dwconv_silu",
    )(xbc, w, bias)


def _dt_kernel(dt_ref, bias_ref, alog_ref, dto_ref, acs_ref):
    raw = dt_ref[0] + bias_ref[...]
    dt = jnp.maximum(raw, 0.0) + jnp.log1p(jnp.exp(-jnp.abs(raw)))
    a = dt * (-jnp.exp(alog_ref[...]))
    li = lax.broadcasted_iota(jnp.int32, (CHUNK, CHUNK), 0)
    si = lax.broadcasted_iota(jnp.int32, (CHUNK, CHUNK), 1)
    lower = jnp.where(li >= si, 1.0, 0.0).astype(F32)
    upper = jnp.where(li <= si, 1.0, 0.0).astype(F32)
    pre = jnp.dot(lower, a, preferred_element_type=F32, precision=HIGHEST)
    suf = jnp.dot(upper, a, preferred_element_type=F32, precision=HIGHEST)
    lane = lax.broadcasted_iota(jnp.int32, (1, DT_W), 1)
    dto_ref[0] = dt
    acs_ref[0] = jnp.where(lane < SSM_HEADS, pre, suf)


def _dt_prep(dt_raw, dt_bias, a_log):
    b, l, _ = dt_raw.shape
    spec = pl.BlockSpec((1, CHUNK, DT_W), lambda i, c: (i, c, 0))
    vec = pl.BlockSpec((1, DT_W), lambda i, c: (0, 0))
    return pl.pallas_call(
        _dt_kernel,
        grid=(b, l // CHUNK),
        in_specs=[spec, vec, vec],
        out_specs=[spec, spec],
        out_shape=[jax.ShapeDtypeStruct((b, l, DT_W), F32)] * 2,
        compiler_params=_params("arbitrary", "arbitrary"),
        name="dt_prep",
    )(dt_raw, dt_bias.reshape(1, DT_W), a_log.reshape(1, DT_W))


F32_PIECES = 3
EXPAND_K = F32_PIECES * HEADS_PER_GROUP
GROUPS_PER_TRIP = 2


def _group_layouts(t, direction):
    b, l, _ = t.shape
    t = t[..., direction * SSM_HEADS:(direction + 1) * SSM_HEADS].reshape(b, l, SSM_GROUPS, HEADS_PER_GROUP)
    cols = jnp.tile(jnp.transpose(t, (0, 2, 1, 3)), (1, 1, 1, F32_PIECES))
    return cols, jnp.transpose(t, (0, 2, 3, 1))


def _expand_heads(v3, width):
    lane = lax.broadcasted_iota(jnp.int32, (1, EXPAND_K), 1)
    hi = v3.astype(BF16).astype(F32)
    rest = v3 - hi
    mid = rest.astype(BF16).astype(F32)
    pieces = jnp.where(lane < HEADS_PER_GROUP, hi,
                       jnp.where(lane < 2 * HEADS_PER_GROUP, mid, rest - mid)).astype(BF16)
    shift = width.bit_length() - 1
    r = lax.broadcasted_iota(jnp.int32, (EXPAND_K, HEADS_PER_GROUP * width), 0)
    c = lax.broadcasted_iota(jnp.int32, (EXPAND_K, HEADS_PER_GROUP * width), 1)
    spread = jnp.where((r & (HEADS_PER_GROUP - 1)) == (c >> shift), 1.0, 0.0).astype(BF16)
    return jnp.dot(pieces, spread, preferred_element_type=F32)


def _ssd_kernel(*refs, reverse, emit_y, add_prev):
    it = iter(refs)
    x_ref, b_ref = next(it), next(it)
    c_ref = next(it) if emit_y else None
    dtc_ref, acsc_ref = next(it), next(it)
    acsr_ref = next(it) if emit_y else None
    h0_ref = next(it)
    if add_prev:
        yp_ref, dexp_ref, z_ref, gn_ref = next(it), next(it), next(it), next(it)
    y_ref = next(it) if emit_y else None
    hf_ref = next(it)
    hs_ref = next(it)
    ybuf_ref = next(it) if add_prev else None

    @pl.when(pl.program_id(1) == 0)
    def _():
        hs_ref[...] = h0_ref[0]

    edge = 0 if reverse else CHUNK - 1
    li = lax.broadcasted_iota(jnp.int32, (CHUNK, CHUNK), 0)
    si = lax.broadcasted_iota(jnp.int32, (CHUNK, CHUNK), 1)
    causal = (li <= si) if reverse else (li >= si)
    lo = lax.broadcasted_iota(jnp.int32, (1, 2 * SSM_HEADDIM), 1) < SSM_HEADDIM

    def load(g):
        lanes = pl.ds(pl.multiple_of(g * GROUP_W, GROUP_W), GROUP_W)
        nl = pl.ds(pl.multiple_of(g * D_STATE, D_STATE), D_STATE)
        v = dict(g=g, lanes=lanes, nl=nl, xg=x_ref[0, :, lanes], bmat=b_ref[0, :, nl],
                 acs3=acsc_ref[0, g], dt3=dtc_ref[0, g], hs=hs_ref[g])
        if emit_y:
            v.update(cmat=c_ref[0, :, nl], acs_r=acsr_ref[0, g])
        if add_prev:
            v.update(yp=yp_ref[0, :, lanes], dexp=dexp_ref[:, lanes])
        return v

    def stage1(v):
        acs3 = v["acs3"]
        last3 = acs3[edge:edge + 1, :]
        v["eacs_x"] = _expand_heads(jnp.exp(acs3), SSM_HEADDIM)
        v["dt_x"] = _expand_heads(v["dt3"], SSM_HEADDIM)
        v["wend_x"] = _expand_heads(jnp.exp(last3 - acs3), SSM_HEADDIM)
        if emit_y:
            c16 = v["cmat"].astype(BF16)
            v["col_x"] = _expand_heads(acs3, CHUNK)
            v["cbm"] = lax.dot_general(c16, v["bmat"].astype(BF16), (((1,), (1,)), ((), ())),
                                       preferred_element_type=F32)
            v["y_off"] = jnp.dot(c16, v["hs"].astype(BF16), preferred_element_type=F32)

    def stage2(v):
        g = v["g"]
        wx = v["xg"] * v["dt_x"]
        wxe = wx * v["wend_x"]
        states = jnp.dot(v["bmat"].T.astype(BF16), wxe.astype(BF16), preferred_element_type=F32)
        hs_new = v["eacs_x"][edge:edge + 1, :] * v["hs"] + states
        hs_ref[g] = hs_new
        hf_ref[0, g] = hs_new
        v["wx16"] = wx.astype(BF16)

    def stage3(v):
        parts = []
        for i in range(HEADS_PER_GROUP // 2):
            ms = []
            for j in (2 * i, 2 * i + 1):
                seg = v["col_x"][:, j * CHUNK:(j + 1) * CHUNK] - v["acs_r"][j:j + 1, :]
                dec = jnp.exp(jnp.where(causal, seg, -jnp.inf))
                ms.append((v["cbm"] * dec).astype(BF16))
            wp = v["wx16"][:, i * 128:(i + 1) * 128]
            zero = jnp.zeros_like(wp)
            rhs = jnp.concatenate([jnp.where(lo, wp, zero), jnp.where(lo, zero, wp)], axis=0)
            parts.append(jnp.dot(jnp.concatenate(ms, axis=1), rhs, preferred_element_type=F32))
        y = jnp.concatenate(parts, axis=1) + v["y_off"] * v["eacs_x"]
        if add_prev:
            ybuf_ref[:, v["lanes"]] = (v["yp"] + y) + v["dexp"] * v["xg"]
        else:
            y_ref[0, :, v["lanes"]] = y

    def group_batch(p, carry):
        vs = [load(GROUPS_PER_TRIP * p + k) for k in range(GROUPS_PER_TRIP)]
        for stage in (stage1, stage2) + ((stage3,) if emit_y else ()):
            for v in vs:
                stage(v)
        return carry

    lax.fori_loop(0, SSM_GROUPS // GROUPS_PER_TRIP, group_batch, 0)

    if add_prev:
        z = z_ref[0]
        t = ybuf_ref[...] * (z * jax.nn.sigmoid(z))
        ms = jnp.mean(t * t, axis=-1, keepdims=True)
        y_ref[0] = ((t * lax.rsqrt(ms + EPS)) * gn_ref[...]).astype(y_ref.dtype)


def _ssd_pass(x, bm, cm, dt, acs, h0, direction, emit_y, y_prev=None, d_exp=None, z=None, g_norm=None):
    b, l, _ = x.shape
    nc = l // CHUNK
    reverse = direction == 1
    add_prev = y_prev is not None
    dtc, _ = _group_layouts(dt, direction)
    acsc, acsr = _group_layouts(acs, direction)
    cidx = (lambda c: nc - 1 - c) if reverse else (lambda c: c)
    xspec = pl.BlockSpec((1, CHUNK, D_INNER), lambda i, c: (i, cidx(c), 0))
    bspec = pl.BlockSpec((1, CHUNK, GN_W), lambda i, c: (i, cidx(c), 0))
    colspec = pl.BlockSpec((1, SSM_GROUPS, CHUNK, EXPAND_K), lambda i, c: (i, 0, cidx(c), 0))
    rowspec = pl.BlockSpec((1, SSM_GROUPS, HEADS_PER_GROUP, CHUNK), lambda i, c: (i, 0, 0, cidx(c)))
    hspec = pl.BlockSpec((1, SSM_GROUPS, D_STATE, GROUP_W), lambda i, c: (i, 0, 0, 0))
    in_specs, args = [xspec, bspec], [x, bm]
    if emit_y:
        in_specs.append(bspec)
        args.append(cm)
    in_specs += [colspec, colspec]
    args += [dtc, acsc]
    if emit_y:
        in_specs.append(rowspec)
        args.append(acsr)
    in_specs.append(hspec)
    args.append(h0)
    scratch = [pltpu.VMEM((SSM_GROUPS, D_STATE, GROUP_W), F32)]
    if add_prev:
        vec = pl.BlockSpec((1, D_INNER), lambda i, c: (0, 0))
        in_specs += [xspec, vec, xspec, vec]
        args += [y_prev, d_exp, z, g_norm.reshape(1, D_INNER)]
        scratch.append(pltpu.VMEM((CHUNK, D_INNER), F32))
    out_specs, out_shape = [], []
    if emit_y:
        out_specs.append(xspec)
        out_shape.append(jax.ShapeDtypeStruct((b, l, D_INNER), BF16 if add_prev else F32))
    out_specs.append(hspec)
    out_shape.append(jax.ShapeDtypeStruct((b, SSM_GROUPS, D_STATE, GROUP_W), F32))
    res = pl.pallas_call(
        functools.partial(_ssd_kernel, reverse=reverse, emit_y=emit_y, add_prev=add_prev),
        grid=(b, nc),
        in_specs=in_specs,
        out_specs=out_specs,
        out_shape=out_shape,
        scratch_shapes=scratch,
        compiler_params=_params("arbitrary", "arbitrary"),
        name="ssd_scan",
    )(*args)
    return (res[0], res[1]) if emit_y else (None, res[0])


def _merge_kernel(yn_ref, at_ref, ws_ref, wa_ref, g1_ref, g2_ref, o_ref):
    t1 = jnp.dot(yn_ref[...], ws_ref[...], preferred_element_type=F32)
    t2 = jnp.dot(at_ref[...], wa_ref[...], preferred_element_type=F32)
    o_ref[...] = (jax.nn.sigmoid(g1_ref[...]) * t1 + jax.nn.sigmoid(g2_ref[...]) * t2).astype(o_ref.dtype)


def _merge(yn, attn, w_ssm, w_attn, gates):
    m = yn.shape[0]
    tm = min(512, m)
    tn = 512
    nj = D_MODEL // tn
    return pl.pallas_call(
        _merge_kernel,
        grid=(m // tm, nj),
        in_specs=[pl.BlockSpec((tm, D_INNER), lambda i, j: (i, 0)),
                  pl.BlockSpec((tm, Q_W), lambda i, j: (i, 0)),
                  pl.BlockSpec((D_INNER, tn), lambda i, j: (0, j)),
                  pl.BlockSpec((Q_W, tn), lambda i, j: (0, j)),
                  pl.BlockSpec((tm, tn), lambda i, j: (i, j)),
                  pl.BlockSpec((tm, tn), lambda i, j: (i, nj + j))],
        out_specs=pl.BlockSpec((tm, tn), lambda i, j: (i, j)),
        out_shape=jax.ShapeDtypeStruct((m, D_MODEL), BF16),
        compiler_params=_params("arbitrary", "arbitrary"),
        name="branch_merge",
    )(yn, attn, w_ssm, w_attn, gates, gates)


def _out_proj_kernel(a_ref, w_ref, x_ref, g_ref, o_ref):
    mix = jnp.dot(a_ref[0], w_ref[...], preferred_element_type=F32)
    o_ref[0] = x_ref[0] + g_ref[0] * mix


def _out_proj(merged, w_out, x, gate):
    b, s, d = x.shape
    tm = min(512, s)
    tn = 512
    return pl.pallas_call(
        _out_proj_kernel,
        grid=(b, s // tm, d // tn),
        in_specs=[pl.BlockSpec((1, tm, d), lambda i, r, j: (i, r, 0)),
                  pl.BlockSpec((d, tn), lambda i, r, j: (0, j)),
                  pl.BlockSpec((1, tm, tn), lambda i, r, j: (i, r, j)),
                  pl.BlockSpec((1, 1, tn), lambda i, r, j: (i, 0, j))],
        out_specs=pl.BlockSpec((1, tm, tn), lambda i, r, j: (i, r, j)),
        out_shape=jax.ShapeDtypeStruct((b, s, d), F32),
        compiler_params=_params("arbitrary", "arbitrary", "arbitrary"),
        name="out_proj",
    )(merged, w_out, x, gate)


PEER_TT = 256


def _topk_rows(s, n_rounds):
    rows = s.shape[0]
    rid = lax.broadcasted_iota(jnp.int32, s.shape, 0).astype(F32)
    rank = jnp.full(s.shape, float(n_rounds), F32)
    vals = []
    for r in range(n_rounds):
        m = jnp.max(s, axis=0, keepdims=True)
        idx = jnp.min(jnp.where(s == m, rid, float(rows)), axis=0, keepdims=True)
        hit = rid == idx
        rank = jnp.where(hit, float(r), rank)
        s = jnp.where(hit, -jnp.inf, s)
        vals.append(m)
    return jnp.concatenate(vals, axis=0), rank


def _staircase_counts(v1, v2):
    k = v1.shape[0]
    aid = lax.broadcasted_iota(jnp.int32, v1.shape, 0).astype(F32)
    n = jnp.zeros_like(v1)
    front = v1 + v2[0:1, :]
    picked = []
    for _ in range(k):
        m = jnp.max(front, axis=0, keepdims=True)
        idx = jnp.min(jnp.where(front == m, aid, float(k)), axis=0, keepdims=True)
        hit = aid == idx
        n = jnp.where(hit, n + 1.0, n)
        nxt = jnp.full_like(v1, -jnp.inf)
        for b in range(1, k):
            nxt = jnp.where(n == float(b), v2[b:b + 1, :], nxt)
        front = jnp.where(hit, v1 + nxt, front)
        picked.append(m)
    return n, jnp.concatenate(picked, axis=0)


def _peer_topk_kernel(u_ref, wq_ref, k1_ref, k2_ref, e1_ref, nc_ref, e2_ref, rb_ref, q_scr):
    q_scr[...] = jnp.dot(u_ref[...], wq_ref[...], preferred_element_type=F32)

    def head(h, carry):
        base = pl.multiple_of(h * 2 * KEY_HALF, 2 * KEY_HALF)
        q1 = q_scr[:, pl.ds(base, KEY_HALF)].astype(BF16)
        q2 = q_scr[:, pl.ds(pl.multiple_of(base + KEY_HALF, KEY_HALF), KEY_HALF)].astype(BF16)
        nt = (((1,), (1,)), ((), ()))
        sc1 = lax.dot_general(k1_ref[...], q1, nt, preferred_element_type=F32)
        sc2 = lax.dot_general(k2_ref[...], q2, nt, preferred_element_type=F32)
        v1, r1 = _topk_rows(sc1, PEER_TOPK)
        v2, r2 = _topk_rows(sc2, PEER_TOPK)
        n, top = _staircase_counts(v1, v2)
        z = jnp.sum(jnp.exp(top - top[0:1, :]), axis=0, keepdims=True)
        nc = jnp.zeros_like(sc1)
        for a in range(PEER_TOPK):
            nc = jnp.where(r1 == float(a), n[a:a + 1, :], nc)
        e1_ref[h] = jnp.where(r1 < float(PEER_TOPK), jnp.exp(sc1 - v1[0:1, :]) / z, 0.0)
        nc_ref[h] = nc
        e2_ref[h] = jnp.exp(sc2 - v2[0:1, :])
        rb_ref[h] = r2
        return carry

    lax.fori_loop(0, PEER_HEADS, head, 0)


def _peer_topk(u2, wq, keys1, keys2):
    m, d = u2.shape
    tt = min(PEER_TT, m)
    big = pl.BlockSpec((PEER_HEADS, N_KEYS, tt), lambda i: (0, 0, i))
    bshape = jax.ShapeDtypeStruct((PEER_HEADS, N_KEYS, m), F32)
    kspec = pl.BlockSpec((N_KEYS, KEY_HALF), lambda i: (0, 0))
    return pl.pallas_call(
        _peer_topk_kernel,
        grid=(m // tt,),
        in_specs=[pl.BlockSpec((tt, d), lambda i: (i, 0)),
                  pl.BlockSpec((d, PEER_HEADS * 2 * KEY_HALF), lambda i: (0, 0)),
                  kspec, kspec],
        out_specs=[big, big, big, big],
        out_shape=[bshape, bshape, bshape, bshape],
        scratch_shapes=[pltpu.VMEM((tt, PEER_HEADS * 2 * KEY_HALF), F32)],
        compiler_params=_params("arbitrary"),
        name="peer_topk",
    )(u2, wq, keys1, keys2)


PEER_NI = 8
PEER_DENSE_TT = 512
SQRT_HALF = float(math.sqrt(0.5))


PEER_LANE_CHUNK = 256


def _peer_stage(tile, ut_ref, pu_ref, pvt_ref, e1_ref, nc_ref, e2_ref, rb_ref, acc_ref, ht_w, ht_r):
    def lane_chunk(t, carry):
        lanes = pl.ds(pl.multiple_of(t * PEER_LANE_CHUNK, PEER_LANE_CHUNK), PEER_LANE_CHUNK)
        ht_w[:, lanes] = jnp.dot(pu_ref[...], ut_ref[:, lanes], preferred_element_type=F32)
        gas = []
        for ii in range(PEER_NI):
            i1 = tile * PEER_NI + ii
            hb = ht_r[ii * N_KEYS:(ii + 1) * N_KEYS, lanes]
            act = 0.5 * hb * (1.0 + lax.erf(hb * SQRT_HALF))
            g = jnp.zeros_like(hb)
            for h in range(PEER_HEADS):
                e1c = e1_ref[h, pl.ds(i1, 1), lanes]
                ncc = nc_ref[h, pl.ds(i1, 1), lanes]
                g = g + jnp.where(rb_ref[h, :, lanes] < ncc, e1c * e2_ref[h, :, lanes], 0.0)
            gas.append((g * act).astype(BF16))
        acc_ref[:, lanes] += jnp.dot(pvt_ref[...], jnp.concatenate(gas, axis=0), preferred_element_type=F32)
        return carry

    lax.fori_loop(0, ht_r.shape[1] // PEER_LANE_CHUNK, lane_chunk, 0)


def _peer_dense_kernel(ut_ref, pu_ref, pvt_ref, e1_ref, nc_ref, e2_ref, rb_ref,
                       h_ref, g5_ref, gf_ref, o_ref, acc_ref, ht0_ref, ht1_ref):
    e = pl.program_id(1)
    ne = pl.num_programs(1)

    @pl.when(e == 0)
    def _():
        acc_ref[...] = jnp.zeros_like(acc_ref)
        ht1_ref[...] = jnp.zeros_like(ht1_ref)

    tile = jnp.maximum(e - 1, 0)
    stage = functools.partial(_peer_stage, tile, ut_ref, pu_ref, pvt_ref, e1_ref, nc_ref, e2_ref, rb_ref, acc_ref)

    @pl.when(e % 2 == 0)
    def _():
        stage(ht0_ref, ht1_ref)

    @pl.when(e % 2 == 1)
    def _():
        stage(ht1_ref, ht0_ref)

    @pl.when(e == ne - 1)
    def _():
        hres = h_ref[0] + g5_ref[0] * acc_ref[...].T
        ms = jnp.mean(hres * hres, axis=-1, keepdims=True)
        o_ref[0] = (hres * lax.rsqrt(ms + EPS)) * gf_ref[...]


def _peer_dense(u2t, pu, pvt, e1, nc, e2, rb, h1, gate5, gfinal):
    b, s, d = h1.shape
    m = b * s
    tt = min(PEER_DENSE_TT, s)
    per = s // tt
    te = PEER_NI * N_KEYS
    ne = N_EXPERTS // te
    once = pl.Buffered(1)
    big = pl.BlockSpec((PEER_HEADS, N_KEYS, tt), lambda i, e: (0, 0, i), pipeline_mode=once)
    return pl.pallas_call(
        _peer_dense_kernel,
        grid=(m // tt, ne + 1),
        in_specs=[pl.BlockSpec((d, tt), lambda i, e: (0, i), pipeline_mode=once),
                  pl.BlockSpec((te, d), lambda i, e: (jnp.minimum(e, ne - 1), 0)),
                  pl.BlockSpec((d, te), lambda i, e: (0, jnp.maximum(e - 1, 0))),
                  big, big, big, big,
                  pl.BlockSpec((1, tt, d), lambda i, e: (i // per, i % per, 0), pipeline_mode=once),
                  pl.BlockSpec((1, 1, d), lambda i, e: (i // per, 0, 0)),
                  pl.BlockSpec((1, d), lambda i, e: (0, 0))],
        out_specs=pl.BlockSpec((1, tt, d), lambda i, e: (i // per, i % per, 0)),
        out_shape=jax.ShapeDtypeStruct((b, s, d), F32),
        scratch_shapes=[pltpu.VMEM((d, tt), F32), pltpu.VMEM((te, tt), F32), pltpu.VMEM((te, tt), F32)],
        compiler_params=_params("arbitrary", "arbitrary", vmem=VMEM_LIMIT_PEER_BYTES),
        name="peer_dense",
    )(u2t, pu, pvt, e1, nc, e2, rb, h1, gate5, gfinal.reshape(1, d))


def _transpose_cast_kernel(w_ref, o_ref):
    o_ref[...] = w_ref[...].T.astype(o_ref.dtype)


def _transpose_cast(w):
    e, d = w.shape
    te = 512
    return pl.pallas_call(
        _transpose_cast_kernel,
        grid=(e // te,),
        in_specs=[pl.BlockSpec((te, d), lambda i: (i, 0))],
        out_specs=pl.BlockSpec((d, te), lambda i: (0, i)),
        out_shape=jax.ShapeDtypeStruct((d, e), BF16),
        compiler_params=_params("arbitrary"),
        name="transpose_cast",
    )(w)


def _rope_tables(rows):
    row = jnp.repeat(jnp.arange(rows), GRID_W).astype(F32)
    col = jnp.tile(jnp.arange(GRID_W), rows).astype(F32)
    freqs = ROPE_BASE ** (-jnp.arange(ROPE_PAIRS, dtype=F32) / ROPE_PAIRS)
    ar = row[:, None] * freqs
    ac = col[:, None] * freqs
    ang = jnp.concatenate([ar, ar, ac, ac], axis=-1)
    cos, sin = jnp.cos(ang), jnp.sin(ang)
    lane = jnp.arange(HEAD_DIM)
    first = (lane % (2 * ROPE_PAIRS)) < ROPE_PAIRS
    return cos, jnp.where(first, -sin, 0.0), jnp.where(first, 0.0, sin)


def kernel(x, c, ctx, c_ctx, ada_w, ada_b, norm1_g, w_in, conv_w, conv_b, dt_bias, a_log, ssm_d, ssm_norm_g,
           attn_sink, w_branch_ssm, w_branch_attn, w_out, norm2_g, peer_wq, peer_keys1, peer_keys2, peer_u,
           peer_v, final_norm_g):
    b, s, d = x.shape
    cl = ctx.shape[1]
    m = b * s
    l = 0

    cc = jnp.concatenate([c, c_ctx[None, :], jnp.zeros((8 - b - 1, d), F32)], axis=0)
    mod_all = _ada(cc, ada_w[l], ada_b[l][None, :])
    mod = mod_all[:b].reshape(b, 6, 1, d)
    modc = jnp.broadcast_to(mod_all[b].reshape(1, 6, 1, d), (b, 6, 1, d))

    (u,) = _norm_mod(x, norm1_g[l], mod[:, 1], mod[:, 0])
    (uc,) = _norm_mod(ctx, norm1_g[l], modc[:, 1], modc[:, 0])
    u = u.reshape(m, d)
    uc = uc.reshape(b * cl, d)

    wi = w_in[l]
    o = 0
    w_k = wi[:, o:o + K_W].astype(BF16); o += K_W
    w_v = wi[:, o:o + K_W].astype(BF16); o += K_W
    w_xbc = wi[:, o:o + XBC_W].astype(BF16); o += XBC_W
    w_dt = wi[:, o:o + DT_W].astype(BF16); o += DT_W
    w_q = wi[:, o:o + Q_W].astype(BF16); o += Q_W
    w_z = wi[:, o:o + Z_W].astype(BF16); o += Z_W
    w_g = wi[:, o:o + GATE_W].astype(BF16)

    rope = _rope_tables(s // GRID_W)
    q = _mm(u, w_q, BF16, rope=rope, seq=s).reshape(b, s, Q_W)
    k = _mm(u, w_k, BF16, rope=rope, seq=s).reshape(b, s, K_W)
    v = _mm(u, w_v, BF16).reshape(b, s, K_W)
    xbc = _mm(u, w_xbc, F32).reshape(b, s, XBC_W)
    dt_raw = _mm(u, w_dt, F32).reshape(b, s, DT_W)
    z = _mm(u, w_z, F32)
    gates = _mm(u, w_g, F32)
    kc = _mm(uc, w_k, BF16).reshape(b, cl, K_W)
    vc = _mm(uc, w_v, BF16).reshape(b, cl, K_W)
    xbcc = _mm(uc, w_xbc, F32).reshape(b, cl, XBC_W)
    dtc_raw = _mm(uc, w_dt, F32).reshape(b, cl, DT_W)

    attn = _attention(q, k, v, kc, vc, attn_sink[l]).reshape(m, Q_W)

    cw, cb = conv_w[l], conv_b[l][None, :]
    xs = _conv_silu(xbc, cw, cb, 0, D_INNER)
    bm = _conv_silu(xbc, cw, cb, D_INNER, GN_W)
    cm = _conv_silu(xbc, cw, cb, D_INNER + GN_W, GN_W)
    xsc = _conv_silu(xbcc, cw, cb, 0, D_INNER)
    bmc = _conv_silu(xbcc, cw, cb, D_INNER, GN_W)
    dt, acs = _dt_prep(dt_raw, dt_bias[l], a_log[l])
    dtc, acsc = _dt_prep(dtc_raw, dt_bias[l], a_log[l])
    hzero = jnp.zeros((b, SSM_GROUPS, D_STATE, GROUP_W), F32)
    _, hcf = _ssd_pass(xsc, bmc, None, dtc, acsc, hzero, 0, False)
    _, hcb = _ssd_pass(xsc, bmc, None, dtc, acsc, hzero, 1, False)
    y_f, _ = _ssd_pass(xs, bm, cm, dt, acs, hcf, 0, True)
    d_exp = jnp.repeat(ssm_d[l], SSM_HEADDIM)[None, :]
    yn, _ = _ssd_pass(xs, bm, cm, dt, acs, hcb, 1, True, y_prev=y_f, d_exp=d_exp,
                      z=z.reshape(b, s, Z_W), g_norm=ssm_norm_g[l])
    yn = yn.reshape(m, D_INNER)
    merged = _merge(yn, attn, w_branch_ssm[l].astype(BF16), w_branch_attn[l].astype(BF16), gates)
    h1 = _out_proj(merged.reshape(b, s, d), w_out[l].astype(BF16), x, mod[:, 2])

    u2, u2t = _norm_mod(h1, norm2_g[l], mod[:, 4], mod[:, 3], transposed=True)
    e1, nc, e2, rb = _peer_topk(u2.reshape(m, d), peer_wq[l].astype(BF16),
                                peer_keys1[l].astype(BF16), peer_keys2[l].astype(BF16))
    pu = peer_u[l].astype(BF16)
    pvt = _transpose_cast(peer_v[l])
    return _peer_dense(u2t, pu, pvt, e1, nc, e2, rb, h1, mod[:, 5], final_norm_g)
```

```python
import functools
import math

import jax
import jax.numpy as jnp
from jax import lax
from jax.experimental import pallas as pl
from jax.experimental.pallas import tpu as pltpu

F32 = jnp.float32
BF16 = jnp.bfloat16
HIGHEST = lax.Precision.HIGHEST

D_MODEL = 2048
GRID_W = 64
EPS = 1e-6
NEG_INF = -1e30
D_INNER = 2 * D_MODEL
SSM_HEADDIM = 64
SSM_HEADS = D_INNER // SSM_HEADDIM
SSM_GROUPS = 8
HEADS_PER_GROUP = SSM_HEADS // SSM_GROUPS
GROUP_W = HEADS_PER_GROUP * SSM_HEADDIM
D_STATE = 128
D_CONV = 5
CHUNK = 128
HEAD_DIM = 128
ATTN_HEADS = D_MODEL // HEAD_DIM
KV_HEADS = 4
Q_PER_KV = ATTN_HEADS // KV_HEADS
ATTN_BLOCK = 128
ROPE_BASE = 10000.0
ROPE_PAIRS = HEAD_DIM // 4
PEER_HEADS = 8
N_KEYS = 128
N_EXPERTS = N_KEYS * N_KEYS
KEY_HALF = 128
PEER_TOPK = 16
K_W = KV_HEADS * HEAD_DIM
GN_W = SSM_GROUPS * D_STATE
XBC_W = D_INNER + 2 * GN_W
DT_W = 2 * SSM_HEADS
Q_W = ATTN_HEADS * HEAD_DIM
Z_W = D_INNER
GATE_W = 2 * D_MODEL
CTX_COLS = 2 * K_W + XBC_W + DT_W

VMEM_LIMIT_BYTES = 52 * 1024 * 1024
VMEM_LIMIT_PEER_BYTES = 58 * 1024 * 1024


def _params(*sem, vmem=VMEM_LIMIT_BYTES):
    return pltpu.CompilerParams(dimension_semantics=sem, vmem_limit_bytes=vmem)


def _ada_kernel(c_ref, w_ref, b_ref, o_ref):
    c = c_ref[...]
    s = c * jax.nn.sigmoid(c)
    o_ref[...] = jnp.dot(s.astype(BF16), w_ref[...].astype(BF16),
                         preferred_element_type=F32) + b_ref[...]


def _ada(cc, w, b):
    rows, d = cc.shape
    n = w.shape[1]
    tn = 1024
    return pl.pallas_call(
        _ada_kernel,
        grid=(n // tn,),
        in_specs=[pl.BlockSpec((rows, d), lambda j: (0, 0)),
                  pl.BlockSpec((d, tn), lambda j: (0, j)),
                  pl.BlockSpec((1, tn), lambda j: (0, j))],
        out_specs=pl.BlockSpec((rows, tn), lambda j: (0, j)),
        out_shape=jax.ShapeDtypeStruct((rows, n), F32),
        compiler_params=_params("arbitrary"),
        name="ada_mod",
    )(cc, w, b)


def _norm_mod_kernel(x_ref, g_ref, sc_ref, sh_ref, o_ref, *t_ref):
    x = x_ref[0]
    ms = jnp.mean(x * x, axis=-1, keepdims=True)
    xn = x * lax.rsqrt(ms + EPS)
    u = (xn * g_ref[...]) * (1.0 + sc_ref[0]) + sh_ref[0]
    o_ref[0] = u.astype(BF16)
    if t_ref:
        t_ref[0][...] = u.T.astype(BF16)


def _norm_mod(x, g, scale, shift, transposed=False):
    b, s, d = x.shape
    ts = min(256, s)
    nt = s // ts
    out_shape = [jax.ShapeDtypeStruct((b, s, d), BF16)]
    out_specs = [pl.BlockSpec((1, ts, d), lambda i, j: (i, j, 0))]
    if transposed:
        out_shape.append(jax.ShapeDtypeStruct((d, b * s), BF16))
        out_specs.append(pl.BlockSpec((d, ts), lambda i, j: (0, i * nt + j)))
    return pl.pallas_call(
        _norm_mod_kernel,
        grid=(b, nt),
        in_specs=[pl.BlockSpec((1, ts, d), lambda i, j: (i, j, 0)),
                  pl.BlockSpec((1, d), lambda i, j: (0, 0)),
                  pl.BlockSpec((1, 1, d), lambda i, j: (i, 0, 0)),
                  pl.BlockSpec((1, 1, d), lambda i, j: (i, 0, 0))],
        out_specs=out_specs,
        out_shape=out_shape,
        compiler_params=_params("arbitrary", "arbitrary"),
        name="norm_mod",
    )(x, g.reshape(1, d), scale, shift)


def _mm_kernel(a_ref, w_ref, o_ref):
    o_ref[...] = jnp.dot(a_ref[...], w_ref[...], preferred_element_type=F32).astype(o_ref.dtype)


def _mm_rope_kernel(a_ref, w_ref, cos_ref, sa_ref, sb_ref, o_ref):
    acc = jnp.dot(a_ref[...], w_ref[...], preferred_element_type=F32)
    cos, sa, sb = cos_ref[...], sa_ref[...], sb_ref[...]
    for j in range(acc.shape[1] // HEAD_DIM):
        t = acc[:, j * HEAD_DIM:(j + 1) * HEAD_DIM]
        r = t * cos + pltpu.roll(t, HEAD_DIM - ROPE_PAIRS, 1) * sa + pltpu.roll(t, ROPE_PAIRS, 1) * sb
        o_ref[:, j * HEAD_DIM:(j + 1) * HEAD_DIM] = r.astype(o_ref.dtype)


def _mm(a, w, out_dtype, rope=None, seq=None):
    m, k = a.shape
    n = w.shape[1]
    tm = min(1024, m if seq is None else seq)
    tn = 1024 if n % 1024 == 0 else min(512, n)
    in_specs = [pl.BlockSpec((tm, k), lambda i, j: (i, 0)),
                pl.BlockSpec((k, tn), lambda i, j: (0, j))]
    args = [a, w]
    kern = _mm_kernel
    if rope is not None:
        per = seq // tm
        in_specs += [pl.BlockSpec((tm, HEAD_DIM), lambda i, j: (i % per, 0))] * 3
        args += list(rope)
        kern = _mm_rope_kernel
    return pl.pallas_call(
        kern,
        grid=(m // tm, n // tn),
        in_specs=in_specs,
        out_specs=pl.BlockSpec((tm, tn), lambda i, j: (i, j)),
        out_shape=jax.ShapeDtypeStruct((m, n), out_dtype),
        compiler_params=_params("arbitrary", "arbitrary"),
        name="in_proj_rope" if rope is not None else "in_proj",
    )(*args)


ATTN_HEADS_PER_STEP = 2


def _attn_kernel(sink_ref, q_ref, kp_ref, kc_ref, kn_ref, vp_ref, vc_ref, vn_ref, kx_ref, vx_ref, o_ref):
    hg = pl.program_id(1)
    n = pl.program_id(2)
    nb = pl.num_programs(2)
    blk = ATTN_BLOCK
    nq = Q_PER_KV * blk
    nk = 3 * blk + kx_ref.shape[1]
    qi = lax.broadcasted_iota(jnp.int32, (nq, nk), 0) & (blk - 1)
    kj = lax.broadcasted_iota(jnp.int32, (nq, nk), 1)
    off_prev = jnp.where(n > 0, 0, 4 * blk)
    off_next = jnp.where(n < nb - 1, 0, 4 * blk)
    bad_prev = (kj < blk) & (kj < qi + off_prev)
    bad_next = (kj >= 2 * blk) & (kj < 3 * blk) & (kj - 2 * blk > qi - off_next)
    bad = bad_prev | bad_next
    rr = lax.broadcasted_iota(jnp.int32, (nq, 1), 0) // blk
    heads = range(ATTN_HEADS_PER_STEP)

    def head_cols(ref, i):
        return ref[0, :, i * HEAD_DIM:(i + 1) * HEAD_DIM]

    scores = []
    for i in heads:
        q = q_ref[0, :, i * Q_PER_KV * HEAD_DIM:(i + 1) * Q_PER_KV * HEAD_DIM]
        qs = jnp.concatenate([q[:, r * HEAD_DIM:(r + 1) * HEAD_DIM] for r in range(Q_PER_KV)], axis=0)
        kcat = jnp.concatenate([head_cols(kp_ref, i), head_cols(kc_ref, i), head_cols(kn_ref, i),
                                head_cols(kx_ref, i)], axis=0)
        scores.append(lax.dot_general(qs, kcat, (((1,), (1,)), ((), ())), preferred_element_type=F32))
    probs = []
    for i in heads:
        s = jnp.where(bad, NEG_INF, scores[i] * (HEAD_DIM ** -0.5))
        sk = jnp.zeros((nq, 1), F32)
        for r in range(Q_PER_KV):
            sk = jnp.where(rr == r, sink_ref[(hg * ATTN_HEADS_PER_STEP + i) * Q_PER_KV + r], sk)
        mx = jnp.maximum(jnp.max(s, axis=1, keepdims=True), sk)
        e = jnp.exp(s - mx)
        den = jnp.sum(e, axis=1, keepdims=True) + jnp.exp(sk - mx)
        probs.append((e / den).astype(BF16))
    for i in heads:
        vcat = jnp.concatenate([head_cols(vp_ref, i), head_cols(vc_ref, i), head_cols(vn_ref, i),
                                head_cols(vx_ref, i)], axis=0)
        o = jnp.dot(probs[i], vcat, preferred_element_type=F32)
        for r in range(Q_PER_KV):
            c0 = (i * Q_PER_KV + r) * HEAD_DIM
            o_ref[0, :, c0:c0 + HEAD_DIM] = o[r * blk:(r + 1) * blk, :].astype(o_ref.dtype)


def _attention(q, k, v, kc, vc, sink):
    b, s, _ = q.shape
    c = kc.shape[1]
    nb = s // ATTN_BLOCK
    qw = ATTN_HEADS_PER_STEP * Q_PER_KV * HEAD_DIM
    kw = ATTN_HEADS_PER_STEP * HEAD_DIM
    kv_spec = lambda f: pl.BlockSpec((1, ATTN_BLOCK, kw), f)
    prev = lambda bi, h, n: (bi, jnp.maximum(n - 1, 0), h)
    cur = lambda bi, h, n: (bi, n, h)
    nxt = lambda bi, h, n: (bi, jnp.minimum(n + 1, nb - 1), h)
    ctx_spec = pl.BlockSpec((1, c, kw), lambda bi, h, n: (bi, 0, h))
    return pl.pallas_call(
        _attn_kernel,
        grid=(b, KV_HEADS // ATTN_HEADS_PER_STEP, nb),
        in_specs=[pl.BlockSpec(memory_space=pltpu.SMEM),
                  pl.BlockSpec((1, ATTN_BLOCK, qw), lambda bi, h, n: (bi, n, h)),
                  kv_spec(prev), kv_spec(cur), kv_spec(nxt),
                  kv_spec(prev), kv_spec(cur), kv_spec(nxt),
                  ctx_spec, ctx_spec],
        out_specs=pl.BlockSpec((1, ATTN_BLOCK, qw), lambda bi, h, n: (bi, n, h)),
        out_shape=jax.ShapeDtypeStruct((b, s, Q_W), BF16),
        compiler_params=_params("arbitrary", "arbitrary", "arbitrary"),
        name="window_attn",
    )(sink, q, k, k, k, v, v, v, kc, vc)


def _conv_kernel(x_ref, w_ref, b_ref, o_ref):
    x = x_ref[0]
    l = x.shape[0]
    w = w_ref[...]
    t = lax.broadcasted_iota(jnp.int32, (l, 1), 0)
    acc = x * w[D_CONV // 2:D_CONV // 2 + 1, :]
    for j in range(D_CONV):
        off = j - D_CONV // 2
        if off == 0:
            continue
        xs = pltpu.roll(x, (-off) % l, 0)
        ok = (t + off >= 0) & (t + off < l)
        acc = acc + jnp.where(ok, xs, 0.0) * w[j:j + 1, :]
    acc = acc + b_ref[...]
    o_ref[0] = (acc * jax.nn.sigmoid(acc)).astype(o_ref.dtype)


def _conv_silu(xbc, w, bias, col0, width):
    b, l, _ = xbc.shape
    tc = 256
    c0 = col0 // tc
    return pl.pallas_call(
        _conv_kernel,
        grid=(b, width // tc),
        in_specs=[pl.BlockSpec((1, l, tc), lambda i, j: (i, 0, c0 + j)),
                  pl.BlockSpec((D_CONV, tc), lambda i, j: (0, c0 + j)),
                  pl.BlockSpec((1, tc), lambda i, j: (0, c0 + j))],
        out_specs=pl.BlockSpec((1, l, tc), lambda i, j: (i, 0, j)),
        out_shape=jax.ShapeDtypeStruct((b, l, width), F32),
        compiler_params=_params("arbitrary", "arbitrary"),
        name="dwconv_silu",
    )(xbc, w, bias)


def _dt_kernel(dt_ref, bias_ref, alog_ref, dto_ref, acs_ref):
    raw = dt_ref[0] + bias_ref[...]
    dt = jnp.maximum(raw, 0.0) + jnp.log1p(jnp.exp(-jnp.abs(raw)))
    a = dt * (-jnp.exp(alog_ref[...]))
    li = lax.broadcasted_iota(jnp.int32, (CHUNK, CHUNK), 0)
    si = lax.broadcasted_iota(jnp.int32, (CHUNK, CHUNK), 1)
    lower = jnp.where(li >= si, 1.0, 0.0).astype(F32)
    upper = jnp.where(li <= si, 1.0, 0.0).astype(F32)
    pre = jnp.dot(lower, a, preferred_element_type=F32, precision=HIGHEST)
    suf = jnp.dot(upper, a, preferred_element_type=F32, precision=HIGHEST)
    lane = lax.broadcasted_iota(jnp.int32, (1, DT_W), 1)
    dto_ref[0] = dt
    acs_ref[0] = jnp.where(lane < SSM_HEADS, pre, suf)


def _dt_prep(dt_raw, dt_bias, a_log):
    b, l, _ = dt_raw.shape
    spec = pl.BlockSpec((1, CHUNK, DT_W), lambda i, c: (i, c, 0))
    vec = pl.BlockSpec((1, DT_W), lambda i, c: (0, 0))
    return pl.pallas_call(
        _dt_kernel,
        grid=(b, l // CHUNK),
        in_specs=[spec, vec, vec],
        out_specs=[spec, spec],
        out_shape=[jax.ShapeDtypeStruct((b, l, DT_W), F32)] * 2,
        compiler_params=_params("arbitrary", "arbitrary"),
        name="dt_prep",
    )(dt_raw, dt_bias.reshape(1, DT_W), a_log.reshape(1, DT_W))


F32_PIECES = 3
EXPAND_K = F32_PIECES * HEADS_PER_GROUP
GROUPS_PER_TRIP = 2


def _group_layouts(t, direction):
    b, l, _ = t.shape
    t = t[..., direction * SSM_HEADS:(direction + 1) * SSM_HEADS].reshape(b, l, SSM_GROUPS, HEADS_PER_GROUP)
    cols = jnp.tile(jnp.transpose(t, (0, 2, 1, 3)), (1, 1, 1, F32_PIECES))
    return cols, jnp.transpose(t, (0, 2, 3, 1))


def _expand_heads(v3, width):
    lane = lax.broadcasted_iota(jnp.int32, (1, EXPAND_K), 1)
    hi = v3.astype(BF16).astype(F32)
    rest = v3 - hi
    mid = rest.astype(BF16).astype(F32)
    pieces = jnp.where(lane < HEADS_PER_GROUP, hi,
                       jnp.where(lane < 2 * HEADS_PER_GROUP, mid, rest - mid)).astype(BF16)
    shift = width.bit_length() - 1
    r = lax.broadcasted_iota(jnp.int32, (EXPAND_K, HEADS_PER_GROUP * width), 0)
    c = lax.broadcasted_iota(jnp.int32, (EXPAND_K, HEADS_PER_GROUP * width), 1)
    spread = jnp.where((r & (HEADS_PER_GROUP - 1)) == (c >> shift), 1.0, 0.0).astype(BF16)
    return jnp.dot(pieces, spread, preferred_element_type=F32)


def _ssd_kernel(*refs, reverse, emit_y, add_prev):
    it = iter(refs)
    x_ref, b_ref = next(it), next(it)
    c_ref = next(it) if emit_y else None
    dtc_ref, acsc_ref = next(it), next(it)
    acsr_ref = next(it) if emit_y else None
    h0_ref = next(it)
    if add_prev:
        yp_ref, dexp_ref, z_ref, gn_ref = next(it), next(it), next(it), next(it)
    y_ref = next(it) if emit_y else None
    hf_ref = next(it)
    hs_ref = next(it)
    ybuf_ref = next(it) if add_prev else None

    @pl.when(pl.program_id(1) == 0)
    def _():
        hs_ref[...] = h0_ref[0]

    edge = 0 if reverse else CHUNK - 1
    li = lax.broadcasted_iota(jnp.int32, (CHUNK, CHUNK), 0)
    si = lax.broadcasted_iota(jnp.int32, (CHUNK, CHUNK), 1)
    causal = (li <= si) if reverse else (li >= si)
    lo = lax.broadcasted_iota(jnp.int32, (1, 2 * SSM_HEADDIM), 1) < SSM_HEADDIM

    def load(g):
        lanes = pl.ds(pl.multiple_of(g * GROUP_W, GROUP_W), GROUP_W)
        nl = pl.ds(pl.multiple_of(g * D_STATE, D_STATE), D_STATE)
        v = dict(g=g, lanes=lanes, nl=nl, xg=x_ref[0, :, lanes], bmat=b_ref[0, :, nl],
                 acs3=acsc_ref[0, g], dt3=dtc_ref[0, g], hs=hs_ref[g])
        if emit_y:
            v.update(cmat=c_ref[0, :, nl], acs_r=acsr_ref[0, g])
        if add_prev:
            v.update(yp=yp_ref[0, :, lanes], dexp=dexp_ref[:, lanes])
        return v

    def stage1(v):
        acs3 = v["acs3"]
        last3 = acs3[edge:edge + 1, :]
        v["eacs_x"] = _expand_heads(jnp.exp(acs3), SSM_HEADDIM)
        v["dt_x"] = _expand_heads(v["dt3"], SSM_HEADDIM)
        v["wend_x"] = _expand_heads(jnp.exp(last3 - acs3), SSM_HEADDIM)
        if emit_y:
            c16 = v["cmat"].astype(BF16)
            v["col_x"] = _expand_heads(acs3, CHUNK)
            v["cbm"] = lax.dot_general(c16, v["bmat"].astype(BF16), (((1,), (1,)), ((), ())),
                                       preferred_element_type=F32)
            v["y_off"] = jnp.dot(c16, v["hs"].astype(BF16), preferred_element_type=F32)

    def stage2(v):
        g = v["g"]
        wx = v["xg"] * v["dt_x"]
        wxe = wx * v["wend_x"]
        states = jnp.dot(v["bmat"].T.astype(BF16), wxe.astype(BF16), preferred_element_type=F32)
        hs_new = v["eacs_x"][edge:edge + 1, :] * v["hs"] + states
        hs_ref[g] = hs_new
        hf_ref[0, g] = hs_new
        v["wx16"] = wx.astype(BF16)

    def stage3(v):
        parts = []
        for i in range(HEADS_PER_GROUP // 2):
            ms = []
            for j in (2 * i, 2 * i + 1):
                seg = v["col_x"][:, j * CHUNK:(j + 1) * CHUNK] - v["acs_r"][j:j + 1, :]
                dec = jnp.exp(jnp.where(causal, seg, -jnp.inf))
                ms.append((v["cbm"] * dec).astype(BF16))
            wp = v["wx16"][:, i * 128:(i + 1) * 128]
            zero = jnp.zeros_like(wp)
            rhs = jnp.concatenate([jnp.where(lo, wp, zero), jnp.where(lo, zero, wp)], axis=0)
            parts.append(jnp.dot(jnp.concatenate(ms, axis=1), rhs, preferred_element_type=F32))
        y = jnp.concatenate(parts, axis=1) + v["y_off"] * v["eacs_x"]
        if add_prev:
            ybuf_ref[:, v["lanes"]] = (v["yp"] + y) + v["dexp"] * v["xg"]
        else:
            y_ref[0, :, v["lanes"]] = y

    def group_batch(p, carry):
        vs = [load(GROUPS_PER_TRIP * p + k) for k in range(GROUPS_PER_TRIP)]
        for stage in (stage1, stage2) + ((stage3,) if emit_y else ()):
            for v in vs:
                stage(v)
        return carry

    lax.fori_loop(0, SSM_GROUPS // GROUPS_PER_TRIP, group_batch, 0)

    if add_prev:
        z = z_ref[0]
        t = ybuf_ref[...] * (z * jax.nn.sigmoid(z))
        ms = jnp.mean(t * t, axis=-1, keepdims=True)
        y_ref[0] = ((t * lax.rsqrt(ms + EPS)) * gn_ref[...]).astype(y_ref.dtype)


def _ssd_pass(x, bm, cm, dt, acs, h0, direction, emit_y, y_prev=None, d_exp=None, z=None, g_norm=None):
    b, l, _ = x.shape
    nc = l // CHUNK
    reverse = direction == 1
    add_prev = y_prev is not None
    dtc, _ = _group_layouts(dt, direction)
    acsc, acsr = _group_layouts(acs, direction)
    cidx = (lambda c: nc - 1 - c) if reverse else (lambda c: c)
    xspec = pl.BlockSpec((1, CHUNK, D_INNER), lambda i, c: (i, cidx(c), 0))
    bspec = pl.BlockSpec((1, CHUNK, GN_W), lambda i, c: (i, cidx(c), 0))
    colspec = pl.BlockSpec((1, SSM_GROUPS, CHUNK, EXPAND_K), lambda i, c: (i, 0, cidx(c), 0))
    rowspec = pl.BlockSpec((1, SSM_GROUPS, HEADS_PER_GROUP, CHUNK), lambda i, c: (i, 0, 0, cidx(c)))
    hspec = pl.BlockSpec((1, SSM_GROUPS, D_STATE, GROUP_W), lambda i, c: (i, 0, 0, 0))
    in_specs, args = [xspec, bspec], [x, bm]
    if emit_y:
        in_specs.append(bspec)
        args.append(cm)
    in_specs += [colspec, colspec]
    args += [dtc, acsc]
    if emit_y:
        in_specs.append(rowspec)
        args.append(acsr)
    in_specs.append(hspec)
    args.append(h0)
    scratch = [pltpu.VMEM((SSM_GROUPS, D_STATE, GROUP_W), F32)]
    if add_prev:
        vec = pl.BlockSpec((1, D_INNER), lambda i, c: (0, 0))
        in_specs += [xspec, vec, xspec, vec]
        args += [y_prev, d_exp, z, g_norm.reshape(1, D_INNER)]
        scratch.append(pltpu.VMEM((CHUNK, D_INNER), F32))
    out_specs, out_shape = [], []
    if emit_y:
        out_specs.append(xspec)
        out_shape.append(jax.ShapeDtypeStruct((b, l, D_INNER), BF16 if add_prev else F32))
    out_specs.append(hspec)
    out_shape.append(jax.ShapeDtypeStruct((b, SSM_GROUPS, D_STATE, GROUP_W), F32))
    res = pl.pallas_call(
        functools.partial(_ssd_kernel, reverse=reverse, emit_y=emit_y, add_prev=add_prev),
        grid=(b, nc),
        in_specs=in_specs,
        out_specs=out_specs,
        out_shape=out_shape,
        scratch_shapes=scratch,
        compiler_params=_params("arbitrary", "arbitrary"),
        name="ssd_scan",
    )(*args)
    return (res[0], res[1]) if emit_y else (None, res[0])


def _merge_kernel(yn_ref, at_ref, ws_ref, wa_ref, g1_ref, g2_ref, o_ref):
    t1 = jnp.dot(yn_ref[...], ws_ref[...], preferred_element_type=F32)
    t2 = jnp.dot(at_ref[...], wa_ref[...], preferred_element_type=F32)
    o_ref[...] = (jax.nn.sigmoid(g1_ref[...]) * t1 + jax.nn.sigmoid(g2_ref[...]) * t2).astype(o_ref.dtype)


def _merge(yn, attn, w_ssm, w_attn, gates):
    m = yn.shape[0]
    tm = min(512, m)
    tn = 1024
    nj = D_MODEL // tn
    return pl.pallas_call(
        _merge_kernel,
        grid=(m // tm, nj),
        in_specs=[pl.BlockSpec((tm, D_INNER), lambda i, j: (i, 0)),
                  pl.BlockSpec((tm, Q_W), lambda i, j: (i, 0)),
                  pl.BlockSpec((D_INNER, tn), lambda i, j: (0, j)),
                  pl.BlockSpec((Q_W, tn), lambda i, j: (0, j)),
                  pl.BlockSpec((tm, tn), lambda i, j: (i, j)),
                  pl.BlockSpec((tm, tn), lambda i, j: (i, nj + j))],
        out_specs=pl.BlockSpec((tm, tn), lambda i, j: (i, j)),
        out_shape=jax.ShapeDtypeStruct((m, D_MODEL), BF16),
        compiler_params=_params("arbitrary", "arbitrary"),
        name="branch_merge",
    )(yn, attn, w_ssm, w_attn, gates, gates)


def _out_proj_kernel(a_ref, w_ref, x_ref, g_ref, o_ref):
    mix = jnp.dot(a_ref[0], w_ref[...], preferred_element_type=F32)
    o_ref[0] = x_ref[0] + g_ref[0] * mix


def _out_proj(merged, w_out, x, gate):
    b, s, d = x.shape
    tm = min(1024, s)
    tn = 1024
    return pl.pallas_call(
        _out_proj_kernel,
        grid=(b, s // tm, d // tn),
        in_specs=[pl.BlockSpec((1, tm, d), lambda i, r, j: (i, r, 0)),
                  pl.BlockSpec((d, tn), lambda i, r, j: (0, j)),
                  pl.BlockSpec((1, tm, tn), lambda i, r, j: (i, r, j)),
                  pl.BlockSpec((1, 1, tn), lambda i, r, j: (i, 0, j))],
        out_specs=pl.BlockSpec((1, tm, tn), lambda i, r, j: (i, r, j)),
        out_shape=jax.ShapeDtypeStruct((b, s, d), F32),
        compiler_params=_params("arbitrary", "arbitrary", "arbitrary"),
        name="out_proj",
    )(merged, w_out, x, gate)


PEER_TT = 256


def _topk_rounds(s, n_rounds, break_ties):
    rows = s.shape[0]
    rid = lax.broadcasted_iota(jnp.int32, s.shape, 0).astype(F32)
    rank = jnp.full(s.shape, float(n_rounds), F32)
    vals = []
    for r in range(n_rounds):
        m = jnp.max(s, axis=0, keepdims=True)
        hit = s == m
        if break_ties:
            hit = rid == jnp.min(jnp.where(hit, rid, float(rows)), axis=0, keepdims=True)
        rank = jnp.where(hit, float(r), rank)
        s = jnp.where(hit, -jnp.inf, s)
        vals.append(m)
    return jnp.concatenate(vals, axis=0), rank


def _topk_rows(scores, n_rounds, scratch):
    worst = None
    for s, (v_scr, r_scr) in zip(scores, scratch):
        v_scr[...], r_scr[...] = _topk_rounds(s, n_rounds, False)
        taken = jnp.sum(jnp.where(r_scr[...] < float(n_rounds), 1.0, 0.0), axis=0, keepdims=True)
        worst = taken if worst is None else jnp.maximum(worst, taken)

    @pl.when(jnp.max(worst) > float(n_rounds))
    def _():
        for s, (v_scr, r_scr) in zip(scores, scratch):
            v_scr[...], r_scr[...] = _topk_rounds(s, n_rounds, True)

    return [(v_scr[...], r_scr[...]) for v_scr, r_scr in scratch]


def _staircase_counts(v1, v2):
    k = v1.shape[0]
    aid = lax.broadcasted_iota(jnp.int32, v1.shape, 0).astype(F32)
    n = jnp.zeros_like(v1)
    front = v1 + v2[0:1, :]
    picked = []
    for _ in range(k):
        m = jnp.max(front, axis=0, keepdims=True)
        idx = jnp.min(jnp.where(front == m, aid, float(k)), axis=0, keepdims=True)
        hit = aid == idx
        n = jnp.where(hit, n + 1.0, n)
        nxt = jnp.full_like(v1, -jnp.inf)
        for b in range(1, k):
            nxt = jnp.where(n == float(b), v2[b:b + 1, :], nxt)
        front = jnp.where(hit, v1 + nxt, front)
        picked.append(m)
    return n, jnp.concatenate(picked, axis=0)


def _peer_topk_kernel(u_ref, wq_ref, k1_ref, k2_ref, e1_ref, nc_ref, e2_ref, rb_ref, q_scr,
                      v1_scr, r1_scr, v2_scr, r2_scr):
    q_scr[...] = jnp.dot(u_ref[...], wq_ref[...], preferred_element_type=F32)

    def head(h, carry):
        base = pl.multiple_of(h * 2 * KEY_HALF, 2 * KEY_HALF)
        q1 = q_scr[:, pl.ds(base, KEY_HALF)].astype(BF16)
        q2 = q_scr[:, pl.ds(pl.multiple_of(base + KEY_HALF, KEY_HALF), KEY_HALF)].astype(BF16)
        nt = (((1,), (1,)), ((), ()))
        sc1 = lax.dot_general(k1_ref[...], q1, nt, preferred_element_type=F32)
        sc2 = lax.dot_general(k2_ref[...], q2, nt, preferred_element_type=F32)
        (v1, r1), (v2, r2) = _topk_rows((sc1, sc2), PEER_TOPK, ((v1_scr, r1_scr), (v2_scr, r2_scr)))
        n, top = _staircase_counts(v1, v2)
        z = jnp.sum(jnp.exp(top - top[0:1, :]), axis=0, keepdims=True)
        nc = jnp.zeros_like(sc1)
        for a in range(PEER_TOPK):
            nc = jnp.where(r1 == float(a), n[a:a + 1, :], nc)
        e1_ref[h] = jnp.where(r1 < float(PEER_TOPK), jnp.exp(sc1 - v1[0:1, :]) / z, 0.0)
        nc_ref[h] = nc
        e2_ref[h] = jnp.exp(sc2 - v2[0:1, :]).astype(e2_ref.dtype)
        rb_ref[h] = r2.astype(rb_ref.dtype)
        return carry

    lax.fori_loop(0, PEER_HEADS, head, 0)


def _peer_topk(u2, wq, keys1, keys2):
    m, d = u2.shape
    tt = min(PEER_TT, m)
    big = pl.BlockSpec((PEER_HEADS, N_KEYS, tt), lambda i: (0, 0, i))
    bshape = jax.ShapeDtypeStruct((PEER_HEADS, N_KEYS, m), F32)
    bshape16 = jax.ShapeDtypeStruct((PEER_HEADS, N_KEYS, m), BF16)
    kspec = pl.BlockSpec((N_KEYS, KEY_HALF), lambda i: (0, 0))
    return pl.pallas_call(
        _peer_topk_kernel,
        grid=(m // tt,),
        in_specs=[pl.BlockSpec((tt, d), lambda i: (i, 0)),
                  pl.BlockSpec((d, PEER_HEADS * 2 * KEY_HALF), lambda i: (0, 0)),
                  kspec, kspec],
        out_specs=[big, big, big, big],
        out_shape=[bshape, bshape, bshape16, bshape16],
        scratch_shapes=[pltpu.VMEM((tt, PEER_HEADS * 2 * KEY_HALF), F32)]
        + [pltpu.VMEM((PEER_TOPK, tt), F32), pltpu.VMEM((N_KEYS, tt), F32)] * 2,
        compiler_params=_params("arbitrary"),
        name="peer_topk",
    )(u2, wq, keys1, keys2)


PEER_NI = 8
PEER_DENSE_TT = 256
SQRT_HALF = float(math.sqrt(0.5))


def _peer_stage(tile, ut_ref, pu_ref, pvt_ref, e1_ref, nc_ref, e2_ref, rb_ref, acc_ref, ht_w, ht_r):
    ht_w[...] = jnp.dot(pu_ref[...], ut_ref[...], preferred_element_type=F32)
    gas = []
    for ii in range(PEER_NI):
        i1 = tile * PEER_NI + ii
        hb = ht_r[ii * N_KEYS:(ii + 1) * N_KEYS, :]
        act = 0.5 * hb * (1.0 + lax.erf(hb * SQRT_HALF))
        g = jnp.zeros(hb.shape, BF16)
        for h in range(PEER_HEADS):
            e1c = e1_ref[h, pl.ds(i1, 1), :].astype(BF16)
            ncc = nc_ref[h, pl.ds(i1, 1), :].astype(BF16)
            g = g + jnp.where(rb_ref[h] < ncc, e1c * e2_ref[h], jnp.zeros((), BF16))
        gas.append(g * act.astype(BF16))
    acc_ref[...] += jnp.dot(pvt_ref[...], jnp.concatenate(gas, axis=0), preferred_element_type=F32)


def _peer_dense_kernel(ut_ref, pu_ref, pvt_ref, e1_ref, nc_ref, e2_ref, rb_ref,
                       h_ref, g5_ref, gf_ref, o_ref, acc_ref, ht0_ref, ht1_ref):
    e = pl.program_id(1)
    ne = pl.num_programs(1)

    @pl.when(e == 0)
    def _():
        acc_ref[...] = jnp.zeros_like(acc_ref)
        ht1_ref[...] = jnp.zeros_like(ht1_ref)

    tile = jnp.maximum(e - 1, 0)
    stage = functools.partial(_peer_stage, tile, ut_ref, pu_ref, pvt_ref, e1_ref, nc_ref, e2_ref, rb_ref, acc_ref)

    @pl.when(e % 2 == 0)
    def _():
        stage(ht0_ref, ht1_ref)

    @pl.when(e % 2 == 1)
    def _():
        stage(ht1_ref, ht0_ref)

    @pl.when(e == ne - 1)
    def _():
        hres = h_ref[0] + g5_ref[0] * acc_ref[...].T
        ms = jnp.mean(hres * hres, axis=-1, keepdims=True)
        o_ref[0] = (hres * lax.rsqrt(ms + EPS)) * gf_ref[...]


def _peer_dense(u2t, pu, pvt, e1, nc, e2, rb, h1, gate5, gfinal):
    b, s, d = h1.shape
    m = b * s
    tt = min(PEER_DENSE_TT, s)
    per = s // tt
    te = PEER_NI * N_KEYS
    ne = N_EXPERTS // te
    once = pl.Buffered(1)
    big = pl.BlockSpec((PEER_HEADS, N_KEYS, tt), lambda i, e: (0, 0, i), pipeline_mode=once)
    return pl.pallas_call(
        _peer_dense_kernel,
        grid=(m // tt, ne + 1),
        in_specs=[pl.BlockSpec((d, tt), lambda i, e: (0, i), pipeline_mode=once),
                  pl.BlockSpec((te, d), lambda i, e: (jnp.minimum(e, ne - 1), 0)),
                  pl.BlockSpec((d, te), lambda i, e: (0, jnp.maximum(e - 1, 0))),
                  big, big, big, big,
                  pl.BlockSpec((1, tt, d), lambda i, e: (i // per, i % per, 0), pipeline_mode=once),
                  pl.BlockSpec((1, 1, d), lambda i, e: (i // per, 0, 0)),
                  pl.BlockSpec((1, d), lambda i, e: (0, 0))],
        out_specs=pl.BlockSpec((1, tt, d), lambda i, e: (i // per, i % per, 0)),
        out_shape=jax.ShapeDtypeStruct((b, s, d), F32),
        scratch_shapes=[pltpu.VMEM((d, tt), F32), pltpu.VMEM((te, tt), F32), pltpu.VMEM((te, tt), F32)],
        compiler_params=_params("arbitrary", "arbitrary", vmem=VMEM_LIMIT_PEER_BYTES),
        name="peer_dense",
    )(u2t, pu, pvt, e1, nc, e2, rb, h1, gate5, gfinal.reshape(1, d))


def _transpose_cast_kernel(w_ref, o_ref):
    o_ref[...] = w_ref[...].T.astype(o_ref.dtype)


def _transpose_cast(w):
    e, d = w.shape
    te = 512
    return pl.pallas_call(
        _transpose_cast_kernel,
        grid=(e // te,),
        in_specs=[pl.BlockSpec((te, d), lambda i: (i, 0))],
        out_specs=pl.BlockSpec((d, te), lambda i: (0, i)),
        out_shape=jax.ShapeDtypeStruct((d, e), BF16),
        compiler_params=_params("arbitrary"),
        name="transpose_cast",
    )(w)


def _rope_tables(rows):
    row = jnp.repeat(jnp.arange(rows), GRID_W).astype(F32)
    col = jnp.tile(jnp.arange(GRID_W), rows).astype(F32)
    freqs = ROPE_BASE ** (-jnp.arange(ROPE_PAIRS, dtype=F32) / ROPE_PAIRS)
    ar = row[:, None] * freqs
    ac = col[:, None] * freqs
    ang = jnp.concatenate([ar, ar, ac, ac], axis=-1)
    cos, sin = jnp.cos(ang), jnp.sin(ang)
    lane = jnp.arange(HEAD_DIM)
    first = (lane % (2 * ROPE_PAIRS)) < ROPE_PAIRS
    return cos, jnp.where(first, -sin, 0.0), jnp.where(first, 0.0, sin)


def kernel(x, c, ctx, c_ctx, ada_w, ada_b, norm1_g, w_in, conv_w, conv_b, dt_bias, a_log, ssm_d, ssm_norm_g,
           attn_sink, w_branch_ssm, w_branch_attn, w_out, norm2_g, peer_wq, peer_keys1, peer_keys2, peer_u,
           peer_v, final_norm_g):
    b, s, d = x.shape
    cl = ctx.shape[1]
    m = b * s
    l = 0

    cc = jnp.concatenate([c, c_ctx[None, :], jnp.zeros((8 - b - 1, d), F32)], axis=0)
    mod_all = _ada(cc, ada_w[l], ada_b[l][None, :])
    mod = mod_all[:b].reshape(b, 6, 1, d)
    modc = jnp.broadcast_to(mod_all[b].reshape(1, 6, 1, d), (b, 6, 1, d))

    (u,) = _norm_mod(x, norm1_g[l], mod[:, 1], mod[:, 0])
    (uc,) = _norm_mod(ctx, norm1_g[l], modc[:, 1], modc[:, 0])
    u = u.reshape(m, d)
    uc = uc.reshape(b * cl, d)

    wi = w_in[l]
    o = 0
    w_k = wi[:, o:o + K_W].astype(BF16); o += K_W
    w_v = wi[:, o:o + K_W].astype(BF16); o += K_W
    w_xbc = wi[:, o:o + XBC_W].astype(BF16); o += XBC_W
    w_dt = wi[:, o:o + DT_W].astype(BF16); o += DT_W
    w_q = wi[:, o:o + Q_W].astype(BF16); o += Q_W
    w_z = wi[:, o:o + Z_W].astype(BF16); o += Z_W
    w_g = wi[:, o:o + GATE_W].astype(BF16)

    rope = _rope_tables(s // GRID_W)
    q = _mm(u, w_q, BF16, rope=rope, seq=s).reshape(b, s, Q_W)
    k = _mm(u, w_k, BF16, rope=rope, seq=s).reshape(b, s, K_W)
    v = _mm(u, w_v, BF16).reshape(b, s, K_W)
    xbc = _mm(u, w_xbc, F32).reshape(b, s, XBC_W)
    dt_raw = _mm(u, w_dt, F32).reshape(b, s, DT_W)
    z = _mm(u, w_z, F32)
    gates = _mm(u, w_g, F32)
    kc = _mm(uc, w_k, BF16).reshape(b, cl, K_W)
    vc = _mm(uc, w_v, BF16).reshape(b, cl, K_W)
    xbcc = _mm(uc, w_xbc, F32).reshape(b, cl, XBC_W)
    dtc_raw = _mm(uc, w_dt, F32).reshape(b, cl, DT_W)

    attn = _attention(q, k, v, kc, vc, attn_sink[l]).reshape(m, Q_W)

    cw, cb = conv_w[l], conv_b[l][None, :]
    xs = _conv_silu(xbc, cw, cb, 0, D_INNER)
    bm = _conv_silu(xbc, cw, cb, D_INNER, GN_W)
    cm = _conv_silu(xbc, cw, cb, D_INNER + GN_W, GN_W)
    xsc = _conv_silu(xbcc, cw, cb, 0, D_INNER)
    bmc = _conv_silu(xbcc, cw, cb, D_INNER, GN_W)
    dt, acs = _dt_prep(dt_raw, dt_bias[l], a_log[l])
    dtc, acsc = _dt_prep(dtc_raw, dt_bias[l], a_log[l])
    hzero = jnp.zeros((b, SSM_GROUPS, D_STATE, GROUP_W), F32)
    _, hcf = _ssd_pass(xsc, bmc, None, dtc, acsc, hzero, 0, False)
    _, hcb = _ssd_pass(xsc, bmc, None, dtc, acsc, hzero, 1, False)
    y_f, _ = _ssd_pass(xs, bm, cm, dt, acs, hcf, 0, True)
    d_exp = jnp.repeat(ssm_d[l], SSM_HEADDIM)[None, :]
    yn, _ = _ssd_pass(xs, bm, cm, dt, acs, hcb, 1, True, y_prev=y_f, d_exp=d_exp,
                      z=z.reshape(b, s, Z_W), g_norm=ssm_norm_g[l])
    yn = yn.reshape(m, D_INNER)
    merged = _merge(yn, attn, w_branch_ssm[l].astype(BF16), w_branch_attn[l].astype(BF16), gates)
    h1 = _out_proj(merged.reshape(b, s, d), w_out[l].astype(BF16), x, mod[:, 2])

    u2, u2t = _norm_mod(h1, norm2_g[l], mod[:, 4], mod[:, 3], transposed=True)
    e1, nc, e2, rb = _peer_topk(u2.reshape(m, d), peer_wq[l].astype(BF16),
                                peer_keys1[l].astype(BF16), peer_keys2[l].astype(BF16))
    pu = peer_u[l].astype(BF16)
    pvt = _transpose_cast(peer_v[l])
    return _peer_dense(u2t, pu, pvt, e1, nc, e2, rb, h1, mod[:, 5], final_norm_g)
```

```python
import functools
import math

import jax
import jax.numpy as jnp
from jax import lax
from jax.experimental import pallas as pl
from jax.experimental.pallas import tpu as pltpu

F32 = jnp.float32
BF16 = jnp.bfloat16
HIGHEST = lax.Precision.HIGHEST

D_MODEL = 2048
GRID_W = 64
EPS = 1e-6
NEG_INF = -1e30
D_INNER = 2 * D_MODEL
SSM_HEADDIM = 64
SSM_HEADS = D_INNER // SSM_HEADDIM
SSM_GROUPS = 8
HEADS_PER_GROUP = SSM_HEADS // SSM_GROUPS
GROUP_W = HEADS_PER_GROUP * SSM_HEADDIM
D_STATE = 128
D_CONV = 5
CHUNK = 128
HEAD_DIM = 128
ATTN_HEADS = D_MODEL // HEAD_DIM
KV_HEADS = 4
Q_PER_KV = ATTN_HEADS // KV_HEADS
ATTN_BLOCK = 128
ROPE_BASE = 10000.0
ROPE_PAIRS = HEAD_DIM // 4
PEER_HEADS = 8
N_KEYS = 128
N_EXPERTS = N_KEYS * N_KEYS
KEY_HALF = 128
PEER_TOPK = 16
K_W = KV_HEADS * HEAD_DIM
GN_W = SSM_GROUPS * D_STATE
XBC_W = D_INNER + 2 * GN_W
DT_W = 2 * SSM_HEADS
Q_W = ATTN_HEADS * HEAD_DIM
Z_W = D_INNER
GATE_W = 2 * D_MODEL
CTX_COLS = 2 * K_W + XBC_W + DT_W

VMEM_LIMIT_BYTES = 52 * 1024 * 1024
VMEM_LIMIT_PEER_BYTES = 58 * 1024 * 1024


def _params(*sem, vmem=VMEM_LIMIT_BYTES):
    return pltpu.CompilerParams(dimension_semantics=sem, vmem_limit_bytes=vmem)


def _ada_kernel(c_ref, w_ref, b_ref, o_ref):
    c = c_ref[...]
    s = c * jax.nn.sigmoid(c)
    o_ref[...] = jnp.dot(s.astype(BF16), w_ref[...].astype(BF16),
                         preferred_element_type=F32) + b_ref[...]


def _ada(cc, w, b):
    rows, d = cc.shape
    n = w.shape[1]
    tn = 1024
    return pl.pallas_call(
        _ada_kernel,
        grid=(n // tn,),
        in_specs=[pl.BlockSpec((rows, d), lambda j: (0, 0)),
                  pl.BlockSpec((d, tn), lambda j: (0, j)),
                  pl.BlockSpec((1, tn), lambda j: (0, j))],
        out_specs=pl.BlockSpec((rows, tn), lambda j: (0, j)),
        out_shape=jax.ShapeDtypeStruct((rows, n), F32),
        compiler_params=_params("arbitrary"),
        name="ada_mod",
    )(cc, w, b)


def _norm_mod_kernel(x_ref, g_ref, sc_ref, sh_ref, o_ref, *t_ref):
    x = x_ref[0]
    ms = jnp.mean(x * x, axis=-1, keepdims=True)
    xn = x * lax.rsqrt(ms + EPS)
    u = (xn * g_ref[...]) * (1.0 + sc_ref[0]) + sh_ref[0]
    o_ref[0] = u.astype(BF16)
    if t_ref:
        t_ref[0][...] = u.T.astype(BF16)


def _norm_mod(x, g, scale, shift, transposed=False):
    b, s, d = x.shape
    ts = min(256, s)
    nt = s // ts
    out_shape = [jax.ShapeDtypeStruct((b, s, d), BF16)]
    out_specs = [pl.BlockSpec((1, ts, d), lambda i, j: (i, j, 0))]
    if transposed:
        out_shape.append(jax.ShapeDtypeStruct((d, b * s), BF16))
        out_specs.append(pl.BlockSpec((d, ts), lambda i, j: (0, i * nt + j)))
    return pl.pallas_call(
        _norm_mod_kernel,
        grid=(b, nt),
        in_specs=[pl.BlockSpec((1, ts, d), lambda i, j: (i, j, 0)),
                  pl.BlockSpec((1, d), lambda i, j: (0, 0)),
                  pl.BlockSpec((1, 1, d), lambda i, j: (i, 0, 0)),
                  pl.BlockSpec((1, 1, d), lambda i, j: (i, 0, 0))],
        out_specs=out_specs,
        out_shape=out_shape,
        compiler_params=_params("arbitrary", "arbitrary"),
        name="norm_mod",
    )(x, g.reshape(1, d), scale, shift)


def _mm_kernel(a_ref, w_ref, o_ref):
    o_ref[...] = jnp.dot(a_ref[...], w_ref[...], preferred_element_type=F32).astype(o_ref.dtype)


def _mm_rope_kernel(a_ref, w_ref, cos_ref, sa_ref, sb_ref, o_ref):
    acc = jnp.dot(a_ref[...], w_ref[...], preferred_element_type=F32)
    cos, sa, sb = cos_ref[...], sa_ref[...], sb_ref[...]
    for j in range(acc.shape[1] // HEAD_DIM):
        t = acc[:, j * HEAD_DIM:(j + 1) * HEAD_DIM]
        r = t * cos + pltpu.roll(t, HEAD_DIM - ROPE_PAIRS, 1) * sa + pltpu.roll(t, ROPE_PAIRS, 1) * sb
        o_ref[:, j * HEAD_DIM:(j + 1) * HEAD_DIM] = r.astype(o_ref.dtype)


def _mm(a, w, out_dtype, rope=None, seq=None):
    m, k = a.shape
    n = w.shape[1]
    tm = min(1024, m if seq is None else seq)
    tn = 1024 if n % 1024 == 0 else min(512, n)
    in_specs = [pl.BlockSpec((tm, k), lambda i, j: (i, 0)),
                pl.BlockSpec((k, tn), lambda i, j: (0, j))]
    args = [a, w]
    kern = _mm_kernel
    if rope is not None:
        per = seq // tm
        in_specs += [pl.BlockSpec((tm, HEAD_DIM), lambda i, j: (i % per, 0))] * 3
        args += list(rope)
        kern = _mm_rope_kernel
    return pl.pallas_call(
        kern,
        grid=(m // tm, n // tn),
        in_specs=in_specs,
        out_specs=pl.BlockSpec((tm, tn), lambda i, j: (i, j)),
        out_shape=jax.ShapeDtypeStruct((m, n), out_dtype),
        compiler_params=_params("arbitrary", "arbitrary"),
        name="in_proj_rope" if rope is not None else "in_proj",
    )(*args)


ATTN_HEADS_PER_STEP = 2


def _attn_kernel(sink_ref, q_ref, kp_ref, kc_ref, kn_ref, vp_ref, vc_ref, vn_ref, kx_ref, vx_ref, o_ref):
    hg = pl.program_id(1)
    n = pl.program_id(2)
    nb = pl.num_programs(2)
    blk = ATTN_BLOCK
    nq = Q_PER_KV * blk
    nk = 3 * blk + kx_ref.shape[1]
    qi = lax.broadcasted_iota(jnp.int32, (nq, nk), 0) & (blk - 1)
    kj = lax.broadcasted_iota(jnp.int32, (nq, nk), 1)
    off_prev = jnp.where(n > 0, 0, 4 * blk)
    off_next = jnp.where(n < nb - 1, 0, 4 * blk)
    bad_prev = (kj < blk) & (kj < qi + off_prev)
    bad_next = (kj >= 2 * blk) & (kj < 3 * blk) & (kj - 2 * blk > qi - off_next)
    bad = bad_prev | bad_next
    rr = lax.broadcasted_iota(jnp.int32, (nq, 1), 0) // blk
    heads = range(ATTN_HEADS_PER_STEP)

    def head_cols(ref, i):
        return ref[0, :, i * HEAD_DIM:(i + 1) * HEAD_DIM]

    scores = []
    for i in heads:
        q = q_ref[0, :, i * Q_PER_KV * HEAD_DIM:(i + 1) * Q_PER_KV * HEAD_DIM]
        qs = jnp.concatenate([q[:, r * HEAD_DIM:(r + 1) * HEAD_DIM] for r in range(Q_PER_KV)], axis=0)
        kcat = jnp.concatenate([head_cols(kp_ref, i), head_cols(kc_ref, i), head_cols(kn_ref, i),
                                head_cols(kx_ref, i)], axis=0)
        scores.append(lax.dot_general(qs, kcat, (((1,), (1,)), ((), ())), preferred_element_type=F32))
    probs = []
    for i in heads:
        s = jnp.where(bad, NEG_INF, scores[i] * (HEAD_DIM ** -0.5))
        sk = jnp.zeros((nq, 1), F32)
        for r in range(Q_PER_KV):
            sk = jnp.where(rr == r, sink_ref[(hg * ATTN_HEADS_PER_STEP + i) * Q_PER_KV + r], sk)
        mx = jnp.maximum(jnp.max(s, axis=1, keepdims=True), sk)
        e = jnp.exp(s - mx)
        den = jnp.sum(e, axis=1, keepdims=True) + jnp.exp(sk - mx)
        probs.append((e / den).astype(BF16))
    for i in heads:
        vcat = jnp.concatenate([head_cols(vp_ref, i), head_cols(vc_ref, i), head_cols(vn_ref, i),
                                head_cols(vx_ref, i)], axis=0)
        o = jnp.dot(probs[i], vcat, preferred_element_type=F32)
        for r in range(Q_PER_KV):
            c0 = (i * Q_PER_KV + r) * HEAD_DIM
            o_ref[0, :, c0:c0 + HEAD_DIM] = o[r * blk:(r + 1) * blk, :].astype(o_ref.dtype)


def _attention(q, k, v, kc, vc, sink):
    b, s, _ = q.shape
    c = kc.shape[1]
    nb = s // ATTN_BLOCK
    qw = ATTN_HEADS_PER_STEP * Q_PER_KV * HEAD_DIM
    kw = ATTN_HEADS_PER_STEP * HEAD_DIM
    kv_spec = lambda f: pl.BlockSpec((1, ATTN_BLOCK, kw), f)
    prev = lambda bi, h, n: (bi, jnp.maximum(n - 1, 0), h)
    cur = lambda bi, h, n: (bi, n, h)
    nxt = lambda bi, h, n: (bi, jnp.minimum(n + 1, nb - 1), h)
    ctx_spec = pl.BlockSpec((1, c, kw), lambda bi, h, n: (bi, 0, h))
    return pl.pallas_call(
        _attn_kernel,
        grid=(b, KV_HEADS // ATTN_HEADS_PER_STEP, nb),
        in_specs=[pl.BlockSpec(memory_space=pltpu.SMEM),
                  pl.BlockSpec((1, ATTN_BLOCK, qw), lambda bi, h, n: (bi, n, h)),
                  kv_spec(prev), kv_spec(cur), kv_spec(nxt),
                  kv_spec(prev), kv_spec(cur), kv_spec(nxt),
                  ctx_spec, ctx_spec],
        out_specs=pl.BlockSpec((1, ATTN_BLOCK, qw), lambda bi, h, n: (bi, n, h)),
        out_shape=jax.ShapeDtypeStruct((b, s, Q_W), BF16),
        compiler_params=_params("arbitrary", "arbitrary", "arbitrary"),
        name="window_attn",
    )(sink, q, k, k, k, v, v, v, kc, vc)


def _conv_kernel(x_ref, w_ref, b_ref, o_ref):
    x = x_ref[0]
    l = x.shape[0]
    w = w_ref[...]
    t = lax.broadcasted_iota(jnp.int32, (l, 1), 0)
    acc = x * w[D_CONV // 2:D_CONV // 2 + 1, :]
    for j in range(D_CONV):
        off = j - D_CONV // 2
        if off == 0:
            continue
        xs = pltpu.roll(x, (-off) % l, 0)
        ok = (t + off >= 0) & (t + off < l)
        acc = acc + jnp.where(ok, xs, 0.0) * w[j:j + 1, :]
    acc = acc + b_ref[...]
    o_ref[0] = (acc * jax.nn.sigmoid(acc)).astype(o_ref.dtype)


def _conv_silu(xbc, w, bias, col0, width):
    b, l, _ = xbc.shape
    tc = 256
    c0 = col0 // tc
    return pl.pallas_call(
        _conv_kernel,
        grid=(b, width // tc),
        in_specs=[pl.BlockSpec((1, l, tc), lambda i, j: (i, 0, c0 + j)),
                  pl.BlockSpec((D_CONV, tc), lambda i, j: (0, c0 + j)),
                  pl.BlockSpec((1, tc), lambda i, j: (0, c0 + j))],
        out_specs=pl.BlockSpec((1, l, tc), lambda i, j: (i, 0, j)),
        out_shape=jax.ShapeDtypeStruct((b, l, width), F32),
        compiler_params=_params("arbitrary", "arbitrary"),
        name="dwconv_silu",
    )(xbc, w, bias)


def _dt_kernel(dt_ref, bias_ref, alog_ref, dto_ref, acs_ref):
    raw = dt_ref[0] + bias_ref[...]
    dt = jnp.maximum(raw, 0.0) + jnp.log1p(jnp.exp(-jnp.abs(raw)))
    a = dt * (-jnp.exp(alog_ref[...]))
    li = lax.broadcasted_iota(jnp.int32, (CHUNK, CHUNK), 0)
    si = lax.broadcasted_iota(jnp.int32, (CHUNK, CHUNK), 1)
    lower = jnp.where(li >= si, 1.0, 0.0).astype(F32)
    upper = jnp.where(li <= si, 1.0, 0.0).astype(F32)
    pre = jnp.dot(lower, a, preferred_element_type=F32, precision=HIGHEST)
    suf = jnp.dot(upper, a, preferred_element_type=F32, precision=HIGHEST)
    lane = lax.broadcasted_iota(jnp.int32, (1, DT_W), 1)
    dto_ref[0] = dt
    acs_ref[0] = jnp.where(lane < SSM_HEADS, pre, suf)


def _dt_prep(dt_raw, dt_bias, a_log):
    b, l, _ = dt_raw.shape
    spec = pl.BlockSpec((1, CHUNK, DT_W), lambda i, c: (i, c, 0))
    vec = pl.BlockSpec((1, DT_W), lambda i, c: (0, 0))
    return pl.pallas_call(
        _dt_kernel,
        grid=(b, l // CHUNK),
        in_specs=[spec, vec, vec],
        out_specs=[spec, spec],
        out_shape=[jax.ShapeDtypeStruct((b, l, DT_W), F32)] * 2,
        compiler_params=_params("arbitrary", "arbitrary"),
        name="dt_prep",
    )(dt_raw, dt_bias.reshape(1, DT_W), a_log.reshape(1, DT_W))


F32_PIECES = 3
EXPAND_K = F32_PIECES * HEADS_PER_GROUP
GROUPS_PER_TRIP = 2


def _group_layouts(t, direction):
    b, l, _ = t.shape
    t = t[..., direction * SSM_HEADS:(direction + 1) * SSM_HEADS].reshape(b, l, SSM_GROUPS, HEADS_PER_GROUP)
    cols = jnp.tile(jnp.transpose(t, (0, 2, 1, 3)), (1, 1, 1, F32_PIECES))
    return cols, jnp.transpose(t, (0, 2, 3, 1))


def _expand_heads(v3, width):
    lane = lax.broadcasted_iota(jnp.int32, (1, EXPAND_K), 1)
    hi = v3.astype(BF16).astype(F32)
    rest = v3 - hi
    mid = rest.astype(BF16).astype(F32)
    pieces = jnp.where(lane < HEADS_PER_GROUP, hi,
                       jnp.where(lane < 2 * HEADS_PER_GROUP, mid, rest - mid)).astype(BF16)
    shift = width.bit_length() - 1
    r = lax.broadcasted_iota(jnp.int32, (EXPAND_K, HEADS_PER_GROUP * width), 0)
    c = lax.broadcasted_iota(jnp.int32, (EXPAND_K, HEADS_PER_GROUP * width), 1)
    spread = jnp.where((r & (HEADS_PER_GROUP - 1)) == (c >> shift), 1.0, 0.0).astype(BF16)
    return jnp.dot(pieces, spread, preferred_element_type=F32)


def _ssd_kernel(*refs, reverse, emit_y, add_prev):
    it = iter(refs)
    x_ref, b_ref = next(it), next(it)
    c_ref = next(it) if emit_y else None
    dtc_ref, acsc_ref = next(it), next(it)
    acsr_ref = next(it) if emit_y else None
    h0_ref = next(it)
    if add_prev:
        yp_ref, dexp_ref, z_ref, gn_ref = next(it), next(it), next(it), next(it)
    y_ref = next(it) if emit_y else None
    hf_ref = next(it)
    hs_ref = next(it)
    ybuf_ref = next(it) if add_prev else None

    @pl.when(pl.program_id(1) == 0)
    def _():
        hs_ref[...] = h0_ref[0]

    edge = 0 if reverse else CHUNK - 1
    li = lax.broadcasted_iota(jnp.int32, (CHUNK, CHUNK), 0)
    si = lax.broadcasted_iota(jnp.int32, (CHUNK, CHUNK), 1)
    causal = (li <= si) if reverse else (li >= si)
    lo = lax.broadcasted_iota(jnp.int32, (1, 2 * SSM_HEADDIM), 1) < SSM_HEADDIM

    def load(g):
        lanes = pl.ds(pl.multiple_of(g * GROUP_W, GROUP_W), GROUP_W)
        nl = pl.ds(pl.multiple_of(g * D_STATE, D_STATE), D_STATE)
        v = dict(g=g, lanes=lanes, nl=nl, xg=x_ref[0, :, lanes], bmat=b_ref[0, :, nl],
                 acs3=acsc_ref[0, g], dt3=dtc_ref[0, g], hs=hs_ref[g])
        if emit_y:
            v.update(cmat=c_ref[0, :, nl], acs_r=acsr_ref[0, g])
        if add_prev:
            v.update(yp=yp_ref[0, :, lanes], dexp=dexp_ref[:, lanes])
        return v

    def stage1(v):
        acs3 = v["acs3"]
        last3 = acs3[edge:edge + 1, :]
        v["eacs_x"] = _expand_heads(jnp.exp(acs3), SSM_HEADDIM)
        v["dt_x"] = _expand_heads(v["dt3"], SSM_HEADDIM)
        v["wend_x"] = _expand_heads(jnp.exp(last3 - acs3), SSM_HEADDIM)
        if emit_y:
            c16 = v["cmat"].astype(BF16)
            v["col_x"] = _expand_heads(acs3, CHUNK)
            v["cbm"] = lax.dot_general(c16, v["bmat"].astype(BF16), (((1,), (1,)), ((), ())),
                                       preferred_element_type=F32)
            v["y_off"] = jnp.dot(c16, v["hs"].astype(BF16), preferred_element_type=F32)

    def stage2(v):
        g = v["g"]
        wx = v["xg"] * v["dt_x"]
        wxe = wx * v["wend_x"]
        states = jnp.dot(v["bmat"].T.astype(BF16), wxe.astype(BF16), preferred_element_type=F32)
        hs_new = v["eacs_x"][edge:edge + 1, :] * v["hs"] + states
        hs_ref[g] = hs_new
        hf_ref[0, g] = hs_new
        v["wx16"] = wx.astype(BF16)

    def stage3(v):
        parts = []
        for i in range(HEADS_PER_GROUP // 2):
            ms = []
            for j in (2 * i, 2 * i + 1):
                seg = v["col_x"][:, j * CHUNK:(j + 1) * CHUNK] - v["acs_r"][j:j + 1, :]
                dec = jnp.exp(jnp.where(causal, seg, -jnp.inf))
                ms.append((v["cbm"] * dec).astype(BF16))
            wp = v["wx16"][:, i * 128:(i + 1) * 128]
            zero = jnp.zeros_like(wp)
            rhs = jnp.concatenate([jnp.where(lo, wp, zero), jnp.where(lo, zero, wp)], axis=0)
            parts.append(jnp.dot(jnp.concatenate(ms, axis=1), rhs, preferred_element_type=F32))
        y = jnp.concatenate(parts, axis=1) + v["y_off"] * v["eacs_x"]
        if add_prev:
            ybuf_ref[:, v["lanes"]] = (v["yp"] + y) + v["dexp"] * v["xg"]
        else:
            y_ref[0, :, v["lanes"]] = y

    def group_batch(p, carry):
        vs = [load(GROUPS_PER_TRIP * p + k) for k in range(GROUPS_PER_TRIP)]
        for stage in (stage1, stage2) + ((stage3,) if emit_y else ()):
            for v in vs:
                stage(v)
        return carry

    lax.fori_loop(0, SSM_GROUPS // GROUPS_PER_TRIP, group_batch, 0)

    if add_prev:
        z = z_ref[0]
        t = ybuf_ref[...] * (z * jax.nn.sigmoid(z))
        ms = jnp.mean(t * t, axis=-1, keepdims=True)
        y_ref[0] = ((t * lax.rsqrt(ms + EPS)) * gn_ref[...]).astype(y_ref.dtype)


def _ssd_pass(x, bm, cm, dt, acs, h0, direction, emit_y, y_prev=None, d_exp=None, z=None, g_norm=None):
    b, l, _ = x.shape
    nc = l // CHUNK
    reverse = direction == 1
    add_prev = y_prev is not None
    dtc, _ = _group_layouts(dt, direction)
    acsc, acsr = _group_layouts(acs, direction)
    cidx = (lambda c: nc - 1 - c) if reverse else (lambda c: c)
    xspec = pl.BlockSpec((1, CHUNK, D_INNER), lambda i, c: (i, cidx(c), 0))
    bspec = pl.BlockSpec((1, CHUNK, GN_W), lambda i, c: (i, cidx(c), 0))
    colspec = pl.BlockSpec((1, SSM_GROUPS, CHUNK, EXPAND_K), lambda i, c: (i, 0, cidx(c), 0))
    rowspec = pl.BlockSpec((1, SSM_GROUPS, HEADS_PER_GROUP, CHUNK), lambda i, c: (i, 0, 0, cidx(c)))
    hspec = pl.BlockSpec((1, SSM_GROUPS, D_STATE, GROUP_W), lambda i, c: (i, 0, 0, 0))
    in_specs, args = [xspec, bspec], [x, bm]
    if emit_y:
        in_specs.append(bspec)
        args.append(cm)
    in_specs += [colspec, colspec]
    args += [dtc, acsc]
    if emit_y:
        in_specs.append(rowspec)
        args.append(acsr)
    in_specs.append(hspec)
    args.append(h0)
    scratch = [pltpu.VMEM((SSM_GROUPS, D_STATE, GROUP_W), F32)]
    if add_prev:
        vec = pl.BlockSpec((1, D_INNER), lambda i, c: (0, 0))
        in_specs += [xspec, vec, xspec, vec]
        args += [y_prev, d_exp, z, g_norm.reshape(1, D_INNER)]
        scratch.append(pltpu.VMEM((CHUNK, D_INNER), F32))
    out_specs, out_shape = [], []
    if emit_y:
        out_specs.append(xspec)
        out_shape.append(jax.ShapeDtypeStruct((b, l, D_INNER), BF16 if add_prev else F32))
    out_specs.append(hspec)
    out_shape.append(jax.ShapeDtypeStruct((b, SSM_GROUPS, D_STATE, GROUP_W), F32))
    res = pl.pallas_call(
        functools.partial(_ssd_kernel, reverse=reverse, emit_y=emit_y, add_prev=add_prev),
        grid=(b, nc),
        in_specs=in_specs,
        out_specs=out_specs,
        out_shape=out_shape,
        scratch_shapes=scratch,
        compiler_params=_params("arbitrary", "arbitrary"),
        name="ssd_scan",
    )(*args)
    return (res[0], res[1]) if emit_y else (None, res[0])


def _merge_kernel(yn_ref, at_ref, ws_ref, wa_ref, g1_ref, g2_ref, o_ref):
    t1 = jnp.dot(yn_ref[...], ws_ref[...], preferred_element_type=F32)
    t2 = jnp.dot(at_ref[...], wa_ref[...], preferred_element_type=F32)
    o_ref[...] = (jax.nn.sigmoid(g1_ref[...]) * t1 + jax.nn.sigmoid(g2_ref[...]) * t2).astype(o_ref.dtype)


def _merge(yn, attn, w_ssm, w_attn, gates):
    m = yn.shape[0]
    tm = min(512, m)
    tn = 1024
    nj = D_MODEL // tn
    return pl.pallas_call(
        _merge_kernel,
        grid=(m // tm, nj),
        in_specs=[pl.BlockSpec((tm, D_INNER), lambda i, j: (i, 0)),
                  pl.BlockSpec((tm, Q_W), lambda i, j: (i, 0)),
                  pl.BlockSpec((D_INNER, tn), lambda i, j: (0, j)),
                  pl.BlockSpec((Q_W, tn), lambda i, j: (0, j)),
                  pl.BlockSpec((tm, tn), lambda i, j: (i, j)),
                  pl.BlockSpec((tm, tn), lambda i, j: (i, nj + j))],
        out_specs=pl.BlockSpec((tm, tn), lambda i, j: (i, j)),
        out_shape=jax.ShapeDtypeStruct((m, D_MODEL), BF16),
        compiler_params=_params("arbitrary", "arbitrary"),
        name="branch_merge",
    )(yn, attn, w_ssm, w_attn, gates, gates)


def _out_proj_kernel(a_ref, w_ref, x_ref, g_ref, o_ref):
    mix = jnp.dot(a_ref[0], w_ref[...], preferred_element_type=F32)
    o_ref[0] = x_ref[0] + g_ref[0] * mix


def _out_proj(merged, w_out, x, gate):
    b, s, d = x.shape
    tm = min(1024, s)
    tn = 1024
    return pl.pallas_call(
        _out_proj_kernel,
        grid=(b, s // tm, d // tn),
        in_specs=[pl.BlockSpec((1, tm, d), lambda i, r, j: (i, r, 0)),
                  pl.BlockSpec((d, tn), lambda i, r, j: (0, j)),
                  pl.BlockSpec((1, tm, tn), lambda i, r, j: (i, r, j)),
                  pl.BlockSpec((1, 1, tn), lambda i, r, j: (i, 0, j))],
        out_specs=pl.BlockSpec((1, tm, tn), lambda i, r, j: (i, r, j)),
        out_shape=jax.ShapeDtypeStruct((b, s, d), F32),
        compiler_params=_params("arbitrary", "arbitrary", "arbitrary"),
        name="out_proj",
    )(merged, w_out, x, gate)


PEER_TT = 256


def _topk_rounds(s, n_rounds, break_ties):
    rows = s.shape[0]
    rid = lax.broadcasted_iota(jnp.int32, s.shape, 0).astype(F32)
    rank = jnp.full(s.shape, float(n_rounds), F32)
    vals = []
    for r in range(n_rounds):
        m = jnp.max(s, axis=0, keepdims=True)
        hit = s == m
        if break_ties:
            hit = rid == jnp.min(jnp.where(hit, rid, float(rows)), axis=0, keepdims=True)
        rank = jnp.where(hit, float(r), rank)
        s = jnp.where(hit, -jnp.inf, s)
        vals.append(m)
    return jnp.concatenate(vals, axis=0), rank


def _topk_rows(scores, n_rounds, scratch):
    worst = None
    for s, (v_scr, r_scr) in zip(scores, scratch):
        v_scr[...], r_scr[...] = _topk_rounds(s, n_rounds, False)
        taken = jnp.sum(jnp.where(r_scr[...] < float(n_rounds), 1.0, 0.0), axis=0, keepdims=True)
        worst = taken if worst is None else jnp.maximum(worst, taken)

    @pl.when(jnp.max(worst) > float(n_rounds))
    def _():
        for s, (v_scr, r_scr) in zip(scores, scratch):
            v_scr[...], r_scr[...] = _topk_rounds(s, n_rounds, True)

    return [(v_scr[...], r_scr[...]) for v_scr, r_scr in scratch]


def _staircase_counts(v1, v2):
    k = v1.shape[0]
    aid = lax.broadcasted_iota(jnp.int32, v1.shape, 0).astype(F32)
    n = jnp.zeros_like(v1)
    front = v1 + v2[0:1, :]
    picked = []
    for _ in range(k):
        m = jnp.max(front, axis=0, keepdims=True)
        idx = jnp.min(jnp.where(front == m, aid, float(k)), axis=0, keepdims=True)
        hit = aid == idx
        n = jnp.where(hit, n + 1.0, n)
        nxt = jnp.full_like(v1, -jnp.inf)
        for b in range(1, k):
            nxt = jnp.where(n == float(b), v2[b:b + 1, :], nxt)
        front = jnp.where(hit, v1 + nxt, front)
        picked.append(m)
    return n, jnp.concatenate(picked, axis=0)


def _peer_topk_kernel(u_ref, wq_ref, k1_ref, k2_ref, e1_ref, nc_ref, e2_ref, rb_ref, q_scr,
                      v1_scr, r1_scr, v2_scr, r2_scr):
    q_scr[...] = jnp.dot(u_ref[...], wq_ref[...], preferred_element_type=F32)

    def head(h, carry):
        base = pl.multiple_of(h * 2 * KEY_HALF, 2 * KEY_HALF)
        q1 = q_scr[:, pl.ds(base, KEY_HALF)].astype(BF16)
        q2 = q_scr[:, pl.ds(pl.multiple_of(base + KEY_HALF, KEY_HALF), KEY_HALF)].astype(BF16)
        nt = (((1,), (1,)), ((), ()))
        sc1 = lax.dot_general(k1_ref[...], q1, nt, preferred_element_type=F32)
        sc2 = lax.dot_general(k2_ref[...], q2, nt, preferred_element_type=F32)
        (v1, r1), (v2, r2) = _topk_rows((sc1, sc2), PEER_TOPK, ((v1_scr, r1_scr), (v2_scr, r2_scr)))
        n, top = _staircase_counts(v1, v2)
        z = jnp.sum(jnp.exp(top - top[0:1, :]), axis=0, keepdims=True)
        nc = jnp.zeros_like(sc1)
        for a in range(PEER_TOPK):
            nc = jnp.where(r1 == float(a), n[a:a + 1, :], nc)
        e1_ref[h] = jnp.where(r1 < float(PEER_TOPK), jnp.exp(sc1 - v1[0:1, :]) / z, 0.0)
        nc_ref[h] = nc
        e2_ref[h] = jnp.exp(sc2 - v2[0:1, :]).astype(e2_ref.dtype)
        rb_ref[h] = r2.astype(rb_ref.dtype)
        return carry

    lax.fori_loop(0, PEER_HEADS, head, 0)


def _peer_topk(u2, wq, keys1, keys2):
    m, d = u2.shape
    tt = min(PEER_TT, m)
    big = pl.BlockSpec((PEER_HEADS, N_KEYS, tt), lambda i: (0, 0, i))
    bshape = jax.ShapeDtypeStruct((PEER_HEADS, N_KEYS, m), F32)
    bshape16 = jax.ShapeDtypeStruct((PEER_HEADS, N_KEYS, m), BF16)
    kspec = pl.BlockSpec((N_KEYS, KEY_HALF), lambda i: (0, 0))
    return pl.pallas_call(
        _peer_topk_kernel,
        grid=(m // tt,),
        in_specs=[pl.BlockSpec((tt, d), lambda i: (i, 0)),
                  pl.BlockSpec((d, PEER_HEADS * 2 * KEY_HALF), lambda i: (0, 0)),
                  kspec, kspec],
        out_specs=[big, big, big, big],
        out_shape=[bshape, bshape, bshape16, bshape16],
        scratch_shapes=[pltpu.VMEM((tt, PEER_HEADS * 2 * KEY_HALF), F32)]
        + [pltpu.VMEM((PEER_TOPK, tt), F32), pltpu.VMEM((N_KEYS, tt), F32)] * 2,
        compiler_params=_params("arbitrary"),
        name="peer_topk",
    )(u2, wq, keys1, keys2)


PEER_NI = 8
PEER_DENSE_TT = 512
PEER_LANES = 256
SQRT_HALF = float(math.sqrt(0.5))


def _peer_stage(tile, lanes, ut_ref, pu_ref, pvt_ref, e1_ref, nc_ref, e2_ref, rb_ref, acc_ref, ht_w, ht_r):
    ht_w[:, lanes] = jnp.dot(pu_ref[...], ut_ref[:, lanes], preferred_element_type=F32)
    gas = []
    for ii in range(PEER_NI):
        i1 = tile * PEER_NI + ii
        hb = ht_r[ii * N_KEYS:(ii + 1) * N_KEYS, lanes]
        act = 0.5 * hb * (1.0 + lax.erf(hb * SQRT_HALF))
        g = jnp.zeros(hb.shape, BF16)
        for h in range(PEER_HEADS):
            e1c = e1_ref[h, pl.ds(i1, 1), lanes].astype(BF16)
            ncc = nc_ref[h, pl.ds(i1, 1), lanes].astype(BF16)
            g = g + jnp.where(rb_ref[h, :, lanes] < ncc, e1c * e2_ref[h, :, lanes], jnp.zeros((), BF16))
        gas.append(g * act.astype(BF16))
    acc_ref[:, lanes] += jnp.dot(pvt_ref[0], jnp.concatenate(gas, axis=0), preferred_element_type=F32)


def _peer_dense_kernel(ut_ref, pu_ref, pvt_ref, e1_ref, nc_ref, e2_ref, rb_ref,
                       h_ref, g5_ref, gf_ref, o_ref, acc_ref, ht0_ref, ht1_ref):
    e = pl.program_id(1)
    ne = pl.num_programs(1)
    part = pl.program_id(2)
    tile = jnp.maximum(e - 1, 0)

    for k in range(ut_ref.shape[1] // PEER_LANES):
        lanes = slice(k * PEER_LANES, (k + 1) * PEER_LANES)
        stage = functools.partial(_peer_stage, tile, lanes, ut_ref, pu_ref, pvt_ref, e1_ref, nc_ref, e2_ref,
                                  rb_ref, acc_ref)

        @pl.when((part == k) & (e == 0))
        def _():
            acc_ref[:, lanes] = jnp.zeros((acc_ref.shape[0], PEER_LANES), F32)
            ht1_ref[:, lanes] = jnp.zeros((ht1_ref.shape[0], PEER_LANES), F32)

        @pl.when((part == k) & (e % 2 == 0))
        def _():
            stage(ht0_ref, ht1_ref)

        @pl.when((part == k) & (e % 2 == 1))
        def _():
            stage(ht1_ref, ht0_ref)

        @pl.when((part == k) & (e == ne - 1))
        def _():
            hres = h_ref[0, lanes, :] + g5_ref[0] * acc_ref[:, lanes].T
            ms = jnp.mean(hres * hres, axis=-1, keepdims=True)
            o_ref[0, lanes, :] = (hres * lax.rsqrt(ms + EPS)) * gf_ref[...]


def _peer_dense(u2t, pu, pvt, e1, nc, e2, rb, h1, gate5, gfinal):
    b, s, d = h1.shape
    m = b * s
    tt = min(PEER_DENSE_TT, s)
    per = s // tt
    te = PEER_NI * N_KEYS
    ne = N_EXPERTS // te
    once = pl.Buffered(1)
    big = pl.BlockSpec((PEER_HEADS, N_KEYS, tt), lambda i, e, k: (0, 0, i), pipeline_mode=once)
    return pl.pallas_call(
        _peer_dense_kernel,
        grid=(m // tt, ne + 1, tt // PEER_LANES),
        in_specs=[pl.BlockSpec((d, tt), lambda i, e, k: (0, i), pipeline_mode=once),
                  pl.BlockSpec((te, d), lambda i, e, k: (jnp.minimum(e, ne - 1), 0)),
                  pl.BlockSpec((1, d, te), lambda i, e, k: (jnp.maximum(e - 1, 0), 0, 0)),
                  big, big, big, big,
                  pl.BlockSpec((1, tt, d), lambda i, e, k: (i // per, i % per, 0), pipeline_mode=once),
                  pl.BlockSpec((1, 1, d), lambda i, e, k: (i // per, 0, 0)),
                  pl.BlockSpec((1, d), lambda i, e, k: (0, 0))],
        out_specs=pl.BlockSpec((1, tt, d), lambda i, e, k: (i // per, i % per, 0)),
        out_shape=jax.ShapeDtypeStruct((b, s, d), F32),
        scratch_shapes=[pltpu.VMEM((d, tt), F32), pltpu.VMEM((te, tt), F32), pltpu.VMEM((te, tt), F32)],
        compiler_params=_params("arbitrary", "arbitrary", "arbitrary", vmem=VMEM_LIMIT_PEER_BYTES),
        name="peer_dense",
    )(u2t, pu, pvt, e1, nc, e2, rb, h1, gate5, gfinal.reshape(1, d))


def _transpose_cast_kernel(w_ref, o_ref):
    half = w_ref.shape[0] // 2
    for k in range(2):
        o_ref[0, :, k * half:(k + 1) * half] = w_ref[k * half:(k + 1) * half, :].T.astype(o_ref.dtype)


def _transpose_cast(w, te):
    e, d = w.shape
    return pl.pallas_call(
        _transpose_cast_kernel,
        grid=(e // te,),
        in_specs=[pl.BlockSpec((te, d), lambda i: (i, 0))],
        out_specs=pl.BlockSpec((1, d, te), lambda i: (i, 0, 0)),
        out_shape=jax.ShapeDtypeStruct((e // te, d, te), BF16),
        compiler_params=_params("arbitrary"),
        name="transpose_cast",
    )(w)


def _rope_tables(rows):
    row = jnp.repeat(jnp.arange(rows), GRID_W).astype(F32)
    col = jnp.tile(jnp.arange(GRID_W), rows).astype(F32)
    freqs = ROPE_BASE ** (-jnp.arange(ROPE_PAIRS, dtype=F32) / ROPE_PAIRS)
    ar = row[:, None] * freqs
    ac = col[:, None] * freqs
    ang = jnp.concatenate([ar, ar, ac, ac], axis=-1)
    cos, sin = jnp.cos(ang), jnp.sin(ang)
    lane = jnp.arange(HEAD_DIM)
    first = (lane % (2 * ROPE_PAIRS)) < ROPE_PAIRS
    return cos, jnp.where(first, -sin, 0.0), jnp.where(first, 0.0, sin)


def kernel(x, c, ctx, c_ctx, ada_w, ada_b, norm1_g, w_in, conv_w, conv_b, dt_bias, a_log, ssm_d, ssm_norm_g,
           attn_sink, w_branch_ssm, w_branch_attn, w_out, norm2_g, peer_wq, peer_keys1, peer_keys2, peer_u,
           peer_v, final_norm_g):
    b, s, d = x.shape
    cl = ctx.shape[1]
    m = b * s
    l = 0

    cc = jnp.concatenate([c, c_ctx[None, :], jnp.zeros((8 - b - 1, d), F32)], axis=0)
    mod_all = _ada(cc, ada_w[l], ada_b[l][None, :])
    mod = mod_all[:b].reshape(b, 6, 1, d)
    modc = jnp.broadcast_to(mod_all[b].reshape(1, 6, 1, d), (b, 6, 1, d))

    (u,) = _norm_mod(x, norm1_g[l], mod[:, 1], mod[:, 0])
    (uc,) = _norm_mod(ctx, norm1_g[l], modc[:, 1], modc[:, 0])
    u = u.reshape(m, d)
    uc = uc.reshape(b * cl, d)

    wi = w_in[l]
    o = 0
    w_k = wi[:, o:o + K_W].astype(BF16); o += K_W
    w_v = wi[:, o:o + K_W].astype(BF16); o += K_W
    w_xbc = wi[:, o:o + XBC_W].astype(BF16); o += XBC_W
    w_dt = wi[:, o:o + DT_W].astype(BF16); o += DT_W
    w_q = wi[:, o:o + Q_W].astype(BF16); o += Q_W
    w_z = wi[:, o:o + Z_W].astype(BF16); o += Z_W
    w_g = wi[:, o:o + GATE_W].astype(BF16)

    rope = _rope_tables(s // GRID_W)
    q = _mm(u, w_q, BF16, rope=rope, seq=s).reshape(b, s, Q_W)
    k = _mm(u, w_k, BF16, rope=rope, seq=s).reshape(b, s, K_W)
    v = _mm(u, w_v, BF16).reshape(b, s, K_W)
    xbc = _mm(u, w_xbc, F32).reshape(b, s, XBC_W)
    dt_raw = _mm(u, w_dt, F32).reshape(b, s, DT_W)
    z = _mm(u, w_z, F32)
    gates = _mm(u, w_g, F32)
    kc = _mm(uc, w_k, BF16).reshape(b, cl, K_W)
    vc = _mm(uc, w_v, BF16).reshape(b, cl, K_W)
    xbcc = _mm(uc, w_xbc, F32).reshape(b, cl, XBC_W)
    dtc_raw = _mm(uc, w_dt, F32).reshape(b, cl, DT_W)

    attn = _attention(q, k, v, kc, vc, attn_sink[l]).reshape(m, Q_W)

    cw, cb = conv_w[l], conv_b[l][None, :]
    xs = _conv_silu(xbc, cw, cb, 0, D_INNER)
    bm = _conv_silu(xbc, cw, cb, D_INNER, GN_W)
    cm = _conv_silu(xbc, cw, cb, D_INNER + GN_W, GN_W)
    xsc = _conv_silu(xbcc, cw, cb, 0, D_INNER)
    bmc = _conv_silu(xbcc, cw, cb, D_INNER, GN_W)
    dt, acs = _dt_prep(dt_raw, dt_bias[l], a_log[l])
    dtc, acsc = _dt_prep(dtc_raw, dt_bias[l], a_log[l])
    hzero = jnp.zeros((b, SSM_GROUPS, D_STATE, GROUP_W), F32)
    _, hcf = _ssd_pass(xsc, bmc, None, dtc, acsc, hzero, 0, False)
    _, hcb = _ssd_pass(xsc, bmc, None, dtc, acsc, hzero, 1, False)
    y_f, _ = _ssd_pass(xs, bm, cm, dt, acs, hcf, 0, True)
    d_exp = jnp.repeat(ssm_d[l], SSM_HEADDIM)[None, :]
    yn, _ = _ssd_pass(xs, bm, cm, dt, acs, hcb, 1, True, y_prev=y_f, d_exp=d_exp,
                      z=z.reshape(b, s, Z_W), g_norm=ssm_norm_g[l])
    yn = yn.reshape(m, D_INNER)
    merged = _merge(yn, attn, w_branch_ssm[l].astype(BF16), w_branch_attn[l].astype(BF16), gates)
    h1 = _out_proj(merged.reshape(b, s, d), w_out[l].astype(BF16), x, mod[:, 2])

    u2, u2t = _norm_mod(h1, norm2_g[l], mod[:, 4], mod[:, 3], transposed=True)
    e1, nc, e2, rb = _peer_topk(u2.reshape(m, d), peer_wq[l].astype(BF16),
                                peer_keys1[l].astype(BF16), peer_keys2[l].astype(BF16))
    pu = peer_u[l].astype(BF16)
    pvt = _transpose_cast(peer_v[l], PEER_NI * N_KEYS)
    return _peer_dense(u2t, pu, pvt, e1, nc, e2, rb, h1, mod[:, 5], final_norm_g)
```

```python
import functools
import math

import jax
import jax.numpy as jnp
from jax import lax
from jax.experimental import pallas as pl
from jax.experimental.pallas import tpu as pltpu

F32 = jnp.float32
BF16 = jnp.bfloat16
HIGHEST = lax.Precision.HIGHEST

D_MODEL = 2048
GRID_W = 64
EPS = 1e-6
NEG_INF = -1e30
D_INNER = 2 * D_MODEL
SSM_HEADDIM = 64
SSM_HEADS = D_INNER // SSM_HEADDIM
SSM_GROUPS = 8
HEADS_PER_GROUP = SSM_HEADS // SSM_GROUPS
GROUP_W = HEADS_PER_GROUP * SSM_HEADDIM
D_STATE = 128
D_CONV = 5
CHUNK = 128
HEAD_DIM = 128
ATTN_HEADS = D_MODEL // HEAD_DIM
KV_HEADS = 4
Q_PER_KV = ATTN_HEADS // KV_HEADS
ATTN_BLOCK = 128
ROPE_BASE = 10000.0
ROPE_PAIRS = HEAD_DIM // 4
PEER_HEADS = 8
N_KEYS = 128
N_EXPERTS = N_KEYS * N_KEYS
KEY_HALF = 128
PEER_TOPK = 16
K_W = KV_HEADS * HEAD_DIM
GN_W = SSM_GROUPS * D_STATE
XBC_W = D_INNER + 2 * GN_W
DT_W = 2 * SSM_HEADS
Q_W = ATTN_HEADS * HEAD_DIM
Z_W = D_INNER
GATE_W = 2 * D_MODEL
CTX_COLS = 2 * K_W + XBC_W + DT_W

VMEM_LIMIT_BYTES = 52 * 1024 * 1024
VMEM_LIMIT_PEER_BYTES = 58 * 1024 * 1024


def _params(*sem, vmem=VMEM_LIMIT_BYTES):
    return pltpu.CompilerParams(dimension_semantics=sem, vmem_limit_bytes=vmem)


def _ada_kernel(c_ref, w_ref, b_ref, o_ref):
    c = c_ref[...]
    s = c * jax.nn.sigmoid(c)
    o_ref[...] = jnp.dot(s.astype(BF16), w_ref[...].astype(BF16),
                         preferred_element_type=F32) + b_ref[...]


def _ada(cc, w, b):
    rows, d = cc.shape
    n = w.shape[1]
    tn = 1024
    return pl.pallas_call(
        _ada_kernel,
        grid=(n // tn,),
        in_specs=[pl.BlockSpec((rows, d), lambda j: (0, 0)),
                  pl.BlockSpec((d, tn), lambda j: (0, j)),
                  pl.BlockSpec((1, tn), lambda j: (0, j))],
        out_specs=pl.BlockSpec((rows, tn), lambda j: (0, j)),
        out_shape=jax.ShapeDtypeStruct((rows, n), F32),
        compiler_params=_params("arbitrary"),
        name="ada_mod",
    )(cc, w, b)


def _norm_mod_kernel(x_ref, g_ref, sc_ref, sh_ref, o_ref, *t_ref):
    x = x_ref[0]
    ms = jnp.mean(x * x, axis=-1, keepdims=True)
    xn = x * lax.rsqrt(ms + EPS)
    u = (xn * g_ref[...]) * (1.0 + sc_ref[0]) + sh_ref[0]
    o_ref[0] = u.astype(BF16)
    if t_ref:
        t_ref[0][...] = u.T.astype(BF16)


def _norm_mod(x, g, scale, shift, transposed=False):
    b, s, d = x.shape
    ts = min(256, s)
    nt = s // ts
    out_shape = [jax.ShapeDtypeStruct((b, s, d), BF16)]
    out_specs = [pl.BlockSpec((1, ts, d), lambda i, j: (i, j, 0))]
    if transposed:
        out_shape.append(jax.ShapeDtypeStruct((d, b * s), BF16))
        out_specs.append(pl.BlockSpec((d, ts), lambda i, j: (0, i * nt + j)))
    return pl.pallas_call(
        _norm_mod_kernel,
        grid=(b, nt),
        in_specs=[pl.BlockSpec((1, ts, d), lambda i, j: (i, j, 0)),
                  pl.BlockSpec((1, d), lambda i, j: (0, 0)),
                  pl.BlockSpec((1, 1, d), lambda i, j: (i, 0, 0)),
                  pl.BlockSpec((1, 1, d), lambda i, j: (i, 0, 0))],
        out_specs=out_specs,
        out_shape=out_shape,
        compiler_params=_params("arbitrary", "arbitrary"),
        name="norm_mod",
    )(x, g.reshape(1, d), scale, shift)


def _mm_kernel(a_ref, w_ref, o_ref):
    o_ref[...] = jnp.dot(a_ref[...], w_ref[...], preferred_element_type=F32).astype(o_ref.dtype)


def _mm_rope_kernel(a_ref, w_ref, cos_ref, sa_ref, sb_ref, o_ref):
    acc = jnp.dot(a_ref[...], w_ref[...], preferred_element_type=F32)
    cos, sa, sb = cos_ref[...], sa_ref[...], sb_ref[...]
    for j in range(acc.shape[1] // HEAD_DIM):
        t = acc[:, j * HEAD_DIM:(j + 1) * HEAD_DIM]
        r = t * cos + pltpu.roll(t, HEAD_DIM - ROPE_PAIRS, 1) * sa + pltpu.roll(t, ROPE_PAIRS, 1) * sb
        o_ref[:, j * HEAD_DIM:(j + 1) * HEAD_DIM] = r.astype(o_ref.dtype)


def _mm(a, w, out_dtype, rope=None, seq=None):
    m, k = a.shape
    n = w.shape[1]
    tm = min(1024, m if seq is None else seq)
    tn = 1024 if n % 1024 == 0 else min(512, n)
    in_specs = [pl.BlockSpec((tm, k), lambda i, j: (i, 0)),
                pl.BlockSpec((k, tn), lambda i, j: (0, j))]
    args = [a, w]
    kern = _mm_kernel
    if rope is not None:
        per = seq // tm
        in_specs += [pl.BlockSpec((tm, HEAD_DIM), lambda i, j: (i % per, 0))] * 3
        args += list(rope)
        kern = _mm_rope_kernel
    return pl.pallas_call(
        kern,
        grid=(m // tm, n // tn),
        in_specs=in_specs,
        out_specs=pl.BlockSpec((tm, tn), lambda i, j: (i, j)),
        out_shape=jax.ShapeDtypeStruct((m, n), out_dtype),
        compiler_params=_params("arbitrary", "arbitrary"),
        name="in_proj_rope" if rope is not None else "in_proj",
    )(*args)


ATTN_HEADS_PER_STEP = 4


def _attn_kernel(sink_ref, q_ref, kp_ref, kc_ref, kn_ref, vp_ref, vc_ref, vn_ref, kx_ref, vx_ref, o_ref):
    hg = pl.program_id(1)
    n = pl.program_id(2)
    nb = pl.num_programs(2)
    blk = ATTN_BLOCK
    nq = Q_PER_KV * blk
    nk = 3 * blk + kx_ref.shape[1]
    qi = lax.broadcasted_iota(jnp.int32, (nq, nk), 0) & (blk - 1)
    kj = lax.broadcasted_iota(jnp.int32, (nq, nk), 1)
    off_prev = jnp.where(n > 0, 0, 4 * blk)
    off_next = jnp.where(n < nb - 1, 0, 4 * blk)
    bad_prev = (kj < blk) & (kj < qi + off_prev)
    bad_next = (kj >= 2 * blk) & (kj < 3 * blk) & (kj - 2 * blk > qi - off_next)
    bad = bad_prev | bad_next
    rr = lax.broadcasted_iota(jnp.int32, (nq, 1), 0) // blk
    heads = range(ATTN_HEADS_PER_STEP)

    def head_cols(ref, i):
        return ref[0, :, i * HEAD_DIM:(i + 1) * HEAD_DIM]

    scores = []
    for i in heads:
        q = q_ref[0, :, i * Q_PER_KV * HEAD_DIM:(i + 1) * Q_PER_KV * HEAD_DIM]
        qs = jnp.concatenate([q[:, r * HEAD_DIM:(r + 1) * HEAD_DIM] for r in range(Q_PER_KV)], axis=0)
        kcat = jnp.concatenate([head_cols(kp_ref, i), head_cols(kc_ref, i), head_cols(kn_ref, i),
                                head_cols(kx_ref, i)], axis=0)
        scores.append(lax.dot_general(qs, kcat, (((1,), (1,)), ((), ())), preferred_element_type=F32))
    probs = []
    for i in heads:
        s = jnp.where(bad, NEG_INF, scores[i] * (HEAD_DIM ** -0.5))
        sk = jnp.zeros((nq, 1), F32)
        for r in range(Q_PER_KV):
            sk = jnp.where(rr == r, sink_ref[(hg * ATTN_HEADS_PER_STEP + i) * Q_PER_KV + r], sk)
        mx = jnp.maximum(jnp.max(s, axis=1, keepdims=True), sk)
        e = jnp.exp(s - mx)
        den = jnp.sum(e, axis=1, keepdims=True) + jnp.exp(sk - mx)
        probs.append((e / den).astype(BF16))
    for i in heads:
        vcat = jnp.concatenate([head_cols(vp_ref, i), head_cols(vc_ref, i), head_cols(vn_ref, i),
                                head_cols(vx_ref, i)], axis=0)
        o = jnp.dot(probs[i], vcat, preferred_element_type=F32)
        for r in range(Q_PER_KV):
            c0 = (i * Q_PER_KV + r) * HEAD_DIM
            o_ref[0, :, c0:c0 + HEAD_DIM] = o[r * blk:(r + 1) * blk, :].astype(o_ref.dtype)


def _attention(q, k, v, kc, vc, sink):
    b, s, _ = q.shape
    c = kc.shape[1]
    nb = s // ATTN_BLOCK
    qw = ATTN_HEADS_PER_STEP * Q_PER_KV * HEAD_DIM
    kw = ATTN_HEADS_PER_STEP * HEAD_DIM
    kv_spec = lambda f: pl.BlockSpec((1, ATTN_BLOCK, kw), f)
    prev = lambda bi, h, n: (bi, jnp.maximum(n - 1, 0), h)
    cur = lambda bi, h, n: (bi, n, h)
    nxt = lambda bi, h, n: (bi, jnp.minimum(n + 1, nb - 1), h)
    ctx_spec = pl.BlockSpec((1, c, kw), lambda bi, h, n: (bi, 0, h))
    return pl.pallas_call(
        _attn_kernel,
        grid=(b, KV_HEADS // ATTN_HEADS_PER_STEP, nb),
        in_specs=[pl.BlockSpec(memory_space=pltpu.SMEM),
                  pl.BlockSpec((1, ATTN_BLOCK, qw), lambda bi, h, n: (bi, n, h)),
                  kv_spec(prev), kv_spec(cur), kv_spec(nxt),
                  kv_spec(prev), kv_spec(cur), kv_spec(nxt),
                  ctx_spec, ctx_spec],
        out_specs=pl.BlockSpec((1, ATTN_BLOCK, qw), lambda bi, h, n: (bi, n, h)),
        out_shape=jax.ShapeDtypeStruct((b, s, Q_W), BF16),
        compiler_params=_params("arbitrary", "arbitrary", "arbitrary"),
        name="window_attn",
    )(sink, q, k, k, k, v, v, v, kc, vc)


CONV_HALO = 8
CONV_LANES = 512


def _conv_kernel(prev_ref, cur_ref, next_ref, w_ref, b_ref, *out_refs, widths):
    r = pl.program_id(1)
    rows = cur_ref.shape[1]
    keep_prev = jnp.where(r > 0, 1.0, 0.0)
    keep_next = jnp.where(r < pl.num_programs(1) - 1, 1.0, 0.0)
    col = 0
    for o_ref, width in zip(out_refs, widths):
        for c0 in range(0, width, CONV_LANES):
            lanes = slice(col + c0, col + c0 + CONV_LANES)
            xw = jnp.concatenate([prev_ref[0, :, lanes] * keep_prev, cur_ref[0, :, lanes],
                                  next_ref[0, :, lanes] * keep_next], axis=0)
            acc = b_ref[:, lanes]
            for j in range(D_CONV):
                off = j - D_CONV // 2
                tap = xw if off == 0 else pltpu.roll(xw, (-off) % xw.shape[0], 0)
                acc = acc + tap[CONV_HALO:CONV_HALO + rows, :] * w_ref[j:j + 1, lanes]
            o_ref[0, :, c0:c0 + CONV_LANES] = (acc * jax.nn.sigmoid(acc)).astype(o_ref.dtype)
        col += width


def _conv_silu(xbc, w, bias, widths):
    b, l, c = xbc.shape
    rows = min(256, l)
    per = rows // CONV_HALO
    last = l // CONV_HALO - 1
    return pl.pallas_call(
        functools.partial(_conv_kernel, widths=widths),
        grid=(b, l // rows),
        in_specs=[pl.BlockSpec((1, CONV_HALO, c), lambda i, r: (i, jnp.maximum(r * per - 1, 0), 0)),
                  pl.BlockSpec((1, rows, c), lambda i, r: (i, r, 0)),
                  pl.BlockSpec((1, CONV_HALO, c), lambda i, r: (i, jnp.minimum((r + 1) * per, last), 0)),
                  pl.BlockSpec((D_CONV, c), lambda i, r: (0, 0)),
                  pl.BlockSpec((1, c), lambda i, r: (0, 0))],
        out_specs=[pl.BlockSpec((1, rows, wd), lambda i, r: (i, r, 0)) for wd in widths],
        out_shape=[jax.ShapeDtypeStruct((b, l, wd), F32) for wd in widths],
        compiler_params=_params("arbitrary", "arbitrary"),
        name="dwconv_silu",
    )(xbc, xbc, xbc, w, bias)


def _dt_kernel(dt_ref, bias_ref, alog_ref, dto_ref, acs_ref):
    raw = dt_ref[0] + bias_ref[...]
    dt = jnp.maximum(raw, 0.0) + jnp.log1p(jnp.exp(-jnp.abs(raw)))
    a = dt * (-jnp.exp(alog_ref[...]))
    li = lax.broadcasted_iota(jnp.int32, (CHUNK, CHUNK), 0)
    si = lax.broadcasted_iota(jnp.int32, (CHUNK, CHUNK), 1)
    lower = jnp.where(li >= si, 1.0, 0.0).astype(F32)
    upper = jnp.where(li <= si, 1.0, 0.0).astype(F32)
    pre = jnp.dot(lower, a, preferred_element_type=F32, precision=HIGHEST)
    suf = jnp.dot(upper, a, preferred_element_type=F32, precision=HIGHEST)
    lane = lax.broadcasted_iota(jnp.int32, (1, DT_W), 1)
    dto_ref[0] = dt
    acs_ref[0] = jnp.where(lane < SSM_HEADS, pre, suf)


def _dt_prep(dt_raw, dt_bias, a_log):
    b, l, _ = dt_raw.shape
    spec = pl.BlockSpec((1, CHUNK, DT_W), lambda i, c: (i, c, 0))
    vec = pl.BlockSpec((1, DT_W), lambda i, c: (0, 0))
    return pl.pallas_call(
        _dt_kernel,
        grid=(b, l // CHUNK),
        in_specs=[spec, vec, vec],
        out_specs=[spec, spec],
        out_shape=[jax.ShapeDtypeStruct((b, l, DT_W), F32)] * 2,
        compiler_params=_params("arbitrary", "arbitrary"),
        name="dt_prep",
    )(dt_raw, dt_bias.reshape(1, DT_W), a_log.reshape(1, DT_W))


F32_PIECES = 3
EXPAND_K = F32_PIECES * HEADS_PER_GROUP
GROUPS_PER_TRIP = 2


def _group_layouts(t, direction):
    b, l, _ = t.shape
    t = t[..., direction * SSM_HEADS:(direction + 1) * SSM_HEADS].reshape(b, l, SSM_GROUPS, HEADS_PER_GROUP)
    cols = jnp.tile(jnp.transpose(t, (0, 2, 1, 3)), (1, 1, 1, F32_PIECES))
    return cols, jnp.transpose(t, (0, 2, 3, 1))


def _expand_heads(v3, width):
    lane = lax.broadcasted_iota(jnp.int32, (1, EXPAND_K), 1)
    hi = v3.astype(BF16).astype(F32)
    rest = v3 - hi
    mid = rest.astype(BF16).astype(F32)
    pieces = jnp.where(lane < HEADS_PER_GROUP, hi,
                       jnp.where(lane < 2 * HEADS_PER_GROUP, mid, rest - mid)).astype(BF16)
    shift = width.bit_length() - 1
    r = lax.broadcasted_iota(jnp.int32, (EXPAND_K, HEADS_PER_GROUP * width), 0)
    c = lax.broadcasted_iota(jnp.int32, (EXPAND_K, HEADS_PER_GROUP * width), 1)
    spread = jnp.where((r & (HEADS_PER_GROUP - 1)) == (c >> shift), 1.0, 0.0).astype(BF16)
    return jnp.dot(pieces, spread, preferred_element_type=F32)


def _ssd_kernel(*refs, reverse, emit_y, add_prev):
    it = iter(refs)
    x_ref, b_ref = next(it), next(it)
    c_ref = next(it) if emit_y else None
    dtc_ref, acsc_ref = next(it), next(it)
    acsr_ref = next(it) if emit_y else None
    h0_ref = next(it)
    if add_prev:
        yp_ref, dexp_ref, z_ref, gn_ref = next(it), next(it), next(it), next(it)
    y_ref = next(it) if emit_y else None
    hf_ref = next(it)
    hs_ref = next(it)
    ybuf_ref = next(it) if add_prev else None

    @pl.when(pl.program_id(1) == 0)
    def _():
        hs_ref[...] = h0_ref[0]

    edge = 0 if reverse else CHUNK - 1
    li = lax.broadcasted_iota(jnp.int32, (CHUNK, CHUNK), 0)
    si = lax.broadcasted_iota(jnp.int32, (CHUNK, CHUNK), 1)
    causal = (li <= si) if reverse else (li >= si)
    lo = lax.broadcasted_iota(jnp.int32, (1, 2 * SSM_HEADDIM), 1) < SSM_HEADDIM

    def load(g):
        lanes = pl.ds(pl.multiple_of(g * GROUP_W, GROUP_W), GROUP_W)
        nl = pl.ds(pl.multiple_of(g * D_STATE, D_STATE), D_STATE)
        v = dict(g=g, lanes=lanes, nl=nl, xg=x_ref[0, :, lanes], bmat=b_ref[0, :, nl],
                 acs3=acsc_ref[0, g], dt3=dtc_ref[0, g], hs=hs_ref[g])
        if emit_y:
            v.update(cmat=c_ref[0, :, nl], acs_r=acsr_ref[0, g])
        if add_prev:
            v.update(yp=yp_ref[0, :, lanes], dexp=dexp_ref[:, lanes])
        return v

    def stage1(v):
        acs3 = v["acs3"]
        last3 = acs3[edge:edge + 1, :]
        v["eacs_x"] = _expand_heads(jnp.exp(acs3), SSM_HEADDIM)
        v["dt_x"] = _expand_heads(v["dt3"], SSM_HEADDIM)
        v["wend_x"] = _expand_heads(jnp.exp(last3 - acs3), SSM_HEADDIM)
        if emit_y:
            c16 = v["cmat"].astype(BF16)
            v["col_x"] = _expand_heads(acs3, CHUNK)
            v["cbm"] = lax.dot_general(c16, v["bmat"].astype(BF16), (((1,), (1,)), ((), ())),
                                       preferred_element_type=F32)
            v["y_off"] = jnp.dot(c16, v["hs"].astype(BF16), preferred_element_type=F32)

    def stage2(v):
        g = v["g"]
        wx = v["xg"] * v["dt_x"]
        wxe = wx * v["wend_x"]
        states = jnp.dot(v["bmat"].T.astype(BF16), wxe.astype(BF16), preferred_element_type=F32)
        hs_new = v["eacs_x"][edge:edge + 1, :] * v["hs"] + states
        hs_ref[g] = hs_new
        hf_ref[0, g] = hs_new
        v["wx16"] = wx.astype(BF16)

    def stage3(v):
        parts = []
        for i in range(HEADS_PER_GROUP // 2):
            ms = []
            for j in (2 * i, 2 * i + 1):
                seg = v["col_x"][:, j * CHUNK:(j + 1) * CHUNK] - v["acs_r"][j:j + 1, :]
                dec = jnp.exp(jnp.where(causal, seg, -jnp.inf))
                ms.append((v["cbm"] * dec).astype(BF16))
            wp = v["wx16"][:, i * 128:(i + 1) * 128]
            zero = jnp.zeros_like(wp)
            rhs = jnp.concatenate([jnp.where(lo, wp, zero), jnp.where(lo, zero, wp)], axis=0)
            parts.append(jnp.dot(jnp.concatenate(ms, axis=1), rhs, preferred_element_type=F32))
        y = jnp.concatenate(parts, axis=1) + v["y_off"] * v["eacs_x"]
        if add_prev:
            ybuf_ref[:, v["lanes"]] = (v["yp"] + y) + v["dexp"] * v["xg"]
        else:
            y_ref[0, :, v["lanes"]] = y

    def group_batch(p, carry):
        vs = [load(GROUPS_PER_TRIP * p + k) for k in range(GROUPS_PER_TRIP)]
        for stage in (stage1, stage2) + ((stage3,) if emit_y else ()):
            for v in vs:
                stage(v)
        return carry

    lax.fori_loop(0, SSM_GROUPS // GROUPS_PER_TRIP, group_batch, 0)

    if add_prev:
        z = z_ref[0]
        t = ybuf_ref[...] * (z * jax.nn.sigmoid(z))
        ms = jnp.mean(t * t, axis=-1, keepdims=True)
        y_ref[0] = ((t * lax.rsqrt(ms + EPS)) * gn_ref[...]).astype(y_ref.dtype)


def _ssd_pass(x, bm, cm, dt, acs, h0, direction, emit_y, y_prev=None, d_exp=None, z=None, g_norm=None):
    b, l, _ = x.shape
    nc = l // CHUNK
    reverse = direction == 1
    add_prev = y_prev is not None
    dtc, _ = _group_layouts(dt, direction)
    acsc, acsr = _group_layouts(acs, direction)
    cidx = (lambda c: nc - 1 - c) if reverse else (lambda c: c)
    xspec = pl.BlockSpec((1, CHUNK, D_INNER), lambda i, c: (i, cidx(c), 0))
    bspec = pl.BlockSpec((1, CHUNK, GN_W), lambda i, c: (i, cidx(c), 0))
    colspec = pl.BlockSpec((1, SSM_GROUPS, CHUNK, EXPAND_K), lambda i, c: (i, 0, cidx(c), 0))
    rowspec = pl.BlockSpec((1, SSM_GROUPS, HEADS_PER_GROUP, CHUNK), lambda i, c: (i, 0, 0, cidx(c)))
    hspec = pl.BlockSpec((1, SSM_GROUPS, D_STATE, GROUP_W), lambda i, c: (i, 0, 0, 0))
    in_specs, args = [xspec, bspec], [x, bm]
    if emit_y:
        in_specs.append(bspec)
        args.append(cm)
    in_specs += [colspec, colspec]
    args += [dtc, acsc]
    if emit_y:
        in_specs.append(rowspec)
        args.append(acsr)
    in_specs.append(hspec)
    args.append(h0)
    scratch = [pltpu.VMEM((SSM_GROUPS, D_STATE, GROUP_W), F32)]
    if add_prev:
        vec = pl.BlockSpec((1, D_INNER), lambda i, c: (0, 0))
        in_specs += [xspec, vec, xspec, vec]
        args += [y_prev, d_exp, z, g_norm.reshape(1, D_INNER)]
        scratch.append(pltpu.VMEM((CHUNK, D_INNER), F32))
    out_specs, out_shape = [], []
    if emit_y:
        out_specs.append(xspec)
        out_shape.append(jax.ShapeDtypeStruct((b, l, D_INNER), BF16 if add_prev else F32))
    out_specs.append(hspec)
    out_shape.append(jax.ShapeDtypeStruct((b, SSM_GROUPS, D_STATE, GROUP_W), F32))
    res = pl.pallas_call(
        functools.partial(_ssd_kernel, reverse=reverse, emit_y=emit_y, add_prev=add_prev),
        grid=(b, nc),
        in_specs=in_specs,
        out_specs=out_specs,
        out_shape=out_shape,
        scratch_shapes=scratch,
        compiler_params=_params("arbitrary", "arbitrary"),
        name="ssd_scan",
    )(*args)
    return (res[0], res[1]) if emit_y else (None, res[0])


def _merge_kernel(yn_ref, at_ref, ws_ref, wa_ref, g1_ref, g2_ref, o_ref):
    t1 = jnp.dot(yn_ref[...], ws_ref[...], preferred_element_type=F32)
    t2 = jnp.dot(at_ref[...], wa_ref[...], preferred_element_type=F32)
    o_ref[...] = (jax.nn.sigmoid(g1_ref[...]) * t1 + jax.nn.sigmoid(g2_ref[...]) * t2).astype(o_ref.dtype)


def _merge(yn, attn, w_ssm, w_attn, gates):
    m = yn.shape[0]
    tm = min(512, m)
    tn = 1024
    nj = D_MODEL // tn
    return pl.pallas_call(
        _merge_kernel,
        grid=(m // tm, nj),
        in_specs=[pl.BlockSpec((tm, D_INNER), lambda i, j: (i, 0)),
                  pl.BlockSpec((tm, Q_W), lambda i, j: (i, 0)),
                  pl.BlockSpec((D_INNER, tn), lambda i, j: (0, j)),
                  pl.BlockSpec((Q_W, tn), lambda i, j: (0, j)),
                  pl.BlockSpec((tm, tn), lambda i, j: (i, j)),
                  pl.BlockSpec((tm, tn), lambda i, j: (i, nj + j))],
        out_specs=pl.BlockSpec((tm, tn), lambda i, j: (i, j)),
        out_shape=jax.ShapeDtypeStruct((m, D_MODEL), BF16),
        compiler_params=_params("arbitrary", "arbitrary"),
        name="branch_merge",
    )(yn, attn, w_ssm, w_attn, gates, gates)


def _out_proj_kernel(a_ref, w_ref, x_ref, g_ref, o_ref):
    mix = jnp.dot(a_ref[0], w_ref[...], preferred_element_type=F32)
    o_ref[0] = x_ref[0] + g_ref[0] * mix


def _out_proj(merged, w_out, x, gate):
    b, s, d = x.shape
    tm = min(1024, s)
    tn = 1024
    return pl.pallas_call(
        _out_proj_kernel,
        grid=(b, s // tm, d // tn),
        in_specs=[pl.BlockSpec((1, tm, d), lambda i, r, j: (i, r, 0)),
                  pl.BlockSpec((d, tn), lambda i, r, j: (0, j)),
                  pl.BlockSpec((1, tm, tn), lambda i, r, j: (i, r, j)),
                  pl.BlockSpec((1, 1, tn), lambda i, r, j: (i, 0, j))],
        out_specs=pl.BlockSpec((1, tm, tn), lambda i, r, j: (i, r, j)),
        out_shape=jax.ShapeDtypeStruct((b, s, d), F32),
        compiler_params=_params("arbitrary", "arbitrary", "arbitrary"),
        name="out_proj",
    )(merged, w_out, x, gate)


PEER_TT = 256


def _topk_rounds(s, n_rounds, break_ties):
    rows = s.shape[0]
    rid = lax.broadcasted_iota(jnp.int32, s.shape, 0).astype(F32)
    rank = jnp.full(s.shape, float(n_rounds), F32)
    vals = []
    for r in range(n_rounds):
        m = jnp.max(s, axis=0, keepdims=True)
        hit = s == m
        if break_ties:
            hit = rid == jnp.min(jnp.where(hit, rid, float(rows)), axis=0, keepdims=True)
        rank = jnp.where(hit, float(r), rank)
        s = jnp.where(hit, -jnp.inf, s)
        vals.append(m)
    return jnp.concatenate(vals, axis=0), rank


def _topk_rows(scores, n_rounds, scratch):
    worst = None
    for s, (v_scr, r_scr) in zip(scores, scratch):
        v_scr[...], r_scr[...] = _topk_rounds(s, n_rounds, False)
        taken = jnp.sum(jnp.where(r_scr[...] < float(n_rounds), 1.0, 0.0), axis=0, keepdims=True)
        worst = taken if worst is None else jnp.maximum(worst, taken)

    @pl.when(jnp.max(worst) > float(n_rounds))
    def _():
        for s, (v_scr, r_scr) in zip(scores, scratch):
            v_scr[...], r_scr[...] = _topk_rounds(s, n_rounds, True)

    return [(v_scr[...], r_scr[...]) for v_scr, r_scr in scratch]


def _staircase_counts(v1, v2):
    k = v1.shape[0]
    aid = lax.broadcasted_iota(jnp.int32, v1.shape, 0).astype(F32)
    n = jnp.zeros_like(v1)
    front = v1 + v2[0:1, :]
    picked = []
    for _ in range(k):
        m = jnp.max(front, axis=0, keepdims=True)
        idx = jnp.min(jnp.where(front == m, aid, float(k)), axis=0, keepdims=True)
        hit = aid == idx
        n = jnp.where(hit, n + 1.0, n)
        nxt = jnp.full_like(v1, -jnp.inf)
        for b in range(1, k):
            nxt = jnp.where(n == float(b), v2[b:b + 1, :], nxt)
        front = jnp.where(hit, v1 + nxt, front)
        picked.append(m)
    return n, jnp.concatenate(picked, axis=0)


def _peer_topk_kernel(u_ref, wq_ref, k1_ref, k2_ref, e1_ref, nc_ref, e2_ref, rb_ref, q_scr,
                      v1_scr, r1_scr, v2_scr, r2_scr):
    q_scr[...] = jnp.dot(u_ref[...], wq_ref[...], preferred_element_type=F32)

    def head(h, carry):
        base = pl.multiple_of(h * 2 * KEY_HALF, 2 * KEY_HALF)
        q1 = q_scr[:, pl.ds(base, KEY_HALF)].astype(BF16)
        q2 = q_scr[:, pl.ds(pl.multiple_of(base + KEY_HALF, KEY_HALF), KEY_HALF)].astype(BF16)
        nt = (((1,), (1,)), ((), ()))
        sc1 = lax.dot_general(k1_ref[...], q1, nt, preferred_element_type=F32)
        sc2 = lax.dot_general(k2_ref[...], q2, nt, preferred_element_type=F32)
        (v1, r1), (v2, r2) = _topk_rows((sc1, sc2), PEER_TOPK, ((v1_scr, r1_scr), (v2_scr, r2_scr)))
        n, top = _staircase_counts(v1, v2)
        z = jnp.sum(jnp.exp(top - top[0:1, :]), axis=0, keepdims=True)
        nc = jnp.zeros_like(sc1)
        for a in range(PEER_TOPK):
            nc = jnp.where(r1 == float(a), n[a:a + 1, :], nc)
        e1_ref[h] = jnp.where(r1 < float(PEER_TOPK), jnp.exp(sc1 - v1[0:1, :]) / z, 0.0)
        nc_ref[h] = nc
        e2_ref[h] = jnp.exp(sc2 - v2[0:1, :]).astype(e2_ref.dtype)
        rb_ref[h] = r2.astype(rb_ref.dtype)
        return carry

    lax.fori_loop(0, PEER_HEADS, head, 0)


def _peer_topk(u2, wq, keys1, keys2):
    m, d = u2.shape
    tt = min(PEER_TT, m)
    big = pl.BlockSpec((PEER_HEADS, N_KEYS, tt), lambda i: (0, 0, i))
    bshape = jax.ShapeDtypeStruct((PEER_HEADS, N_KEYS, m), F32)
    bshape16 = jax.ShapeDtypeStruct((PEER_HEADS, N_KEYS, m), BF16)
    kspec = pl.BlockSpec((N_KEYS, KEY_HALF), lambda i: (0, 0))
    return pl.pallas_call(
        _peer_topk_kernel,
        grid=(m // tt,),
        in_specs=[pl.BlockSpec((tt, d), lambda i: (i, 0)),
                  pl.BlockSpec((d, PEER_HEADS * 2 * KEY_HALF), lambda i: (0, 0)),
                  kspec, kspec],
        out_specs=[big, big, big, big],
        out_shape=[bshape, bshape, bshape16, bshape16],
        scratch_shapes=[pltpu.VMEM((tt, PEER_HEADS * 2 * KEY_HALF), F32)]
        + [pltpu.VMEM((PEER_TOPK, tt), F32), pltpu.VMEM((N_KEYS, tt), F32)] * 2,
        compiler_params=_params("arbitrary"),
        name="peer_topk",
    )(u2, wq, keys1, keys2)


PEER_NI = 8
PEER_DENSE_TT = 512
PEER_LANES = 256
SQRT_HALF = float(math.sqrt(0.5))


def _peer_stage(tile, lanes, ut_ref, pu_ref, pvt_ref, e1_ref, nc_ref, e2_ref, rb_ref, acc_ref, ht_w, ht_r):
    ht_w[:, lanes] = jnp.dot(pu_ref[...], ut_ref[:, lanes], preferred_element_type=F32)
    gas = []
    for ii in range(PEER_NI):
        i1 = tile * PEER_NI + ii
        hb = ht_r[ii * N_KEYS:(ii + 1) * N_KEYS, lanes]
        act = 0.5 * hb * (1.0 + lax.erf(hb * SQRT_HALF))
        g = jnp.zeros(hb.shape, BF16)
        for h in range(PEER_HEADS):
            e1c = e1_ref[h, pl.ds(i1, 1), lanes].astype(BF16)
            ncc = nc_ref[h, pl.ds(i1, 1), lanes].astype(BF16)
            g = g + jnp.where(rb_ref[h, :, lanes] < ncc, e1c * e2_ref[h, :, lanes], jnp.zeros((), BF16))
        gas.append(g * act.astype(BF16))
    acc_ref[:, lanes] += jnp.dot(pvt_ref[0], jnp.concatenate(gas, axis=0), preferred_element_type=F32)


def _peer_dense_kernel(ut_ref, pu_ref, pvt_ref, e1_ref, nc_ref, e2_ref, rb_ref,
                       h_ref, g5_ref, gf_ref, o_ref, acc_ref, ht0_ref, ht1_ref):
    e = pl.program_id(1)
    ne = pl.num_programs(1)
    part = pl.program_id(2)
    tile = jnp.maximum(e - 1, 0)

    for k in range(ut_ref.shape[1] // PEER_LANES):
        lanes = slice(k * PEER_LANES, (k + 1) * PEER_LANES)
        stage = functools.partial(_peer_stage, tile, lanes, ut_ref, pu_ref, pvt_ref, e1_ref, nc_ref, e2_ref,
                                  rb_ref, acc_ref)

        @pl.when((part == k) & (e == 0))
        def _():
            acc_ref[:, lanes] = jnp.zeros((acc_ref.shape[0], PEER_LANES), F32)
            ht1_ref[:, lanes] = jnp.zeros((ht1_ref.shape[0], PEER_LANES), F32)

        @pl.when((part == k) & (e % 2 == 0))
        def _():
            stage(ht0_ref, ht1_ref)

        @pl.when((part == k) & (e % 2 == 1))
        def _():
            stage(ht1_ref, ht0_ref)

        @pl.when((part == k) & (e == ne - 1))
        def _():
            hres = h_ref[0, lanes, :] + g5_ref[0] * acc_ref[:, lanes].T
            ms = jnp.mean(hres * hres, axis=-1, keepdims=True)
            o_ref[0, lanes, :] = (hres * lax.rsqrt(ms + EPS)) * gf_ref[...]


def _peer_dense(u2t, pu, pvt, e1, nc, e2, rb, h1, gate5, gfinal):
    b, s, d = h1.shape
    m = b * s
    tt = min(PEER_DENSE_TT, s)
    per = s // tt
    te = PEER_NI * N_KEYS
    ne = N_EXPERTS // te
    once = pl.Buffered(1)
    big = pl.BlockSpec((PEER_HEADS, N_KEYS, tt), lambda i, e, k: (0, 0, i), pipeline_mode=once)
    return pl.pallas_call(
        _peer_dense_kernel,
        grid=(m // tt, ne + 1, tt // PEER_LANES),
        in_specs=[pl.BlockSpec((d, tt), lambda i, e, k: (0, i), pipeline_mode=once),
                  pl.BlockSpec((te, d), lambda i, e, k: (jnp.minimum(e, ne - 1), 0)),
                  pl.BlockSpec((1, d, te), lambda i, e, k: (jnp.maximum(e - 1, 0), 0, 0)),
                  big, big, big, big,
                  pl.BlockSpec((1, tt, d), lambda i, e, k: (i // per, i % per, 0), pipeline_mode=once),
                  pl.BlockSpec((1, 1, d), lambda i, e, k: (i // per, 0, 0)),
                  pl.BlockSpec((1, d), lambda i, e, k: (0, 0))],
        out_specs=pl.BlockSpec((1, tt, d), lambda i, e, k: (i // per, i % per, 0)),
        out_shape=jax.ShapeDtypeStruct((b, s, d), F32),
        scratch_shapes=[pltpu.VMEM((d, tt), F32), pltpu.VMEM((te, tt), F32), pltpu.VMEM((te, tt), F32)],
        compiler_params=_params("arbitrary", "arbitrary", "arbitrary", vmem=VMEM_LIMIT_PEER_BYTES),
        name="peer_dense",
    )(u2t, pu, pvt, e1, nc, e2, rb, h1, gate5, gfinal.reshape(1, d))


def _transpose_cast_kernel(w_ref, o_ref):
    half = w_ref.shape[0] // 2
    for k in range(2):
        o_ref[0, :, k * half:(k + 1) * half] = w_ref[k * half:(k + 1) * half, :].T.astype(o_ref.dtype)


def _transpose_cast(w, te):
    e, d = w.shape
    return pl.pallas_call(
        _transpose_cast_kernel,
        grid=(e // te,),
        in_specs=[pl.BlockSpec((te, d), lambda i: (i, 0))],
        out_specs=pl.BlockSpec((1, d, te), lambda i: (i, 0, 0)),
        out_shape=jax.ShapeDtypeStruct((e // te, d, te), BF16),
        compiler_params=_params("arbitrary"),
        name="transpose_cast",
    )(w)


def _rope_tables(rows):
    row = jnp.repeat(jnp.arange(rows), GRID_W).astype(F32)
    col = jnp.tile(jnp.arange(GRID_W), rows).astype(F32)
    freqs = ROPE_BASE ** (-jnp.arange(ROPE_PAIRS, dtype=F32) / ROPE_PAIRS)
    ar = row[:, None] * freqs
    ac = col[:, None] * freqs
    ang = jnp.concatenate([ar, ar, ac, ac], axis=-1)
    cos, sin = jnp.cos(ang), jnp.sin(ang)
    lane = jnp.arange(HEAD_DIM)
    first = (lane % (2 * ROPE_PAIRS)) < ROPE_PAIRS
    return cos, jnp.where(first, -sin, 0.0), jnp.where(first, 0.0, sin)


def kernel(x, c, ctx, c_ctx, ada_w, ada_b, norm1_g, w_in, conv_w, conv_b, dt_bias, a_log, ssm_d, ssm_norm_g,
           attn_sink, w_branch_ssm, w_branch_attn, w_out, norm2_g, peer_wq, peer_keys1, peer_keys2, peer_u,
           peer_v, final_norm_g):
    b, s, d = x.shape
    cl = ctx.shape[1]
    m = b * s
    l = 0

    cc = jnp.concatenate([c, c_ctx[None, :], jnp.zeros((8 - b - 1, d), F32)], axis=0)
    mod_all = _ada(cc, ada_w[l], ada_b[l][None, :])
    mod = mod_all[:b].reshape(b, 6, 1, d)
    modc = jnp.broadcast_to(mod_all[b].reshape(1, 6, 1, d), (b, 6, 1, d))

    (u,) = _norm_mod(x, norm1_g[l], mod[:, 1], mod[:, 0])
    (uc,) = _norm_mod(ctx, norm1_g[l], modc[:, 1], modc[:, 0])
    u = u.reshape(m, d)
    uc = uc.reshape(b * cl, d)

    wi = w_in[l]
    o = 0
    w_k = wi[:, o:o + K_W].astype(BF16); o += K_W
    w_v = wi[:, o:o + K_W].astype(BF16); o += K_W
    w_xbc = wi[:, o:o + XBC_W].astype(BF16); o += XBC_W
    w_dt = wi[:, o:o + DT_W].astype(BF16); o += DT_W
    w_q = wi[:, o:o + Q_W].astype(BF16); o += Q_W
    w_z = wi[:, o:o + Z_W].astype(BF16); o += Z_W
    w_g = wi[:, o:o + GATE_W].astype(BF16)

    rope = _rope_tables(s // GRID_W)
    q = _mm(u, w_q, BF16, rope=rope, seq=s).reshape(b, s, Q_W)
    k = _mm(u, w_k, BF16, rope=rope, seq=s).reshape(b, s, K_W)
    v = _mm(u, w_v, BF16).reshape(b, s, K_W)
    xbc = _mm(u, w_xbc, F32).reshape(b, s, XBC_W)
    dt_raw = _mm(u, w_dt, F32).reshape(b, s, DT_W)
    z = _mm(u, w_z, F32)
    gates = _mm(u, w_g, F32)
    kc = _mm(uc, w_k, BF16).reshape(b, cl, K_W)
    vc = _mm(uc, w_v, BF16).reshape(b, cl, K_W)
    xbcc = _mm(uc, w_xbc, F32).reshape(b, cl, XBC_W)
    dtc_raw = _mm(uc, w_dt, F32).reshape(b, cl, DT_W)

    attn = _attention(q, k, v, kc, vc, attn_sink[l]).reshape(m, Q_W)

    cw, cb = conv_w[l], conv_b[l][None, :]
    xs, bm, cm = _conv_silu(xbc, cw, cb, (D_INNER, GN_W, GN_W))
    xsc, bmc = _conv_silu(xbcc, cw, cb, (D_INNER, GN_W))
    dt, acs = _dt_prep(dt_raw, dt_bias[l], a_log[l])
    dtc, acsc = _dt_prep(dtc_raw, dt_bias[l], a_log[l])
    hzero = jnp.zeros((b, SSM_GROUPS, D_STATE, GROUP_W), F32)
    _, hcf = _ssd_pass(xsc, bmc, None, dtc, acsc, hzero, 0, False)
    _, hcb = _ssd_pass(xsc, bmc, None, dtc, acsc, hzero, 1, False)
    y_f, _ = _ssd_pass(xs, bm, cm, dt, acs, hcf, 0, True)
    d_exp = jnp.repeat(ssm_d[l], SSM_HEADDIM)[None, :]
    yn, _ = _ssd_pass(xs, bm, cm, dt, acs, hcb, 1, True, y_prev=y_f, d_exp=d_exp,
                      z=z.reshape(b, s, Z_W), g_norm=ssm_norm_g[l])
    yn = yn.reshape(m, D_INNER)
    merged = _merge(yn, attn, w_branch_ssm[l].astype(BF16), w_branch_attn[l].astype(BF16), gates)
    h1 = _out_proj(merged.reshape(b, s, d), w_out[l].astype(BF16), x, mod[:, 2])

    u2, u2t = _norm_mod(h1, norm2_g[l], mod[:, 4], mod[:, 3], transposed=True)
    e1, nc, e2, rb = _peer_topk(u2.reshape(m, d), peer_wq[l].astype(BF16),
                                peer_keys1[l].astype(BF16), peer_keys2[l].astype(BF16))
    pu = peer_u[l].astype(BF16)
    pvt = _transpose_cast(peer_v[l], PEER_NI * N_KEYS)
    return _peer_dense(u2t, pu, pvt, e1, nc, e2, rb, h1, mod[:, 5], final_norm_g)
```

```python
import functools
import math

import jax
import jax.numpy as jnp
from jax import lax
from jax.experimental import pallas as pl
from jax.experimental.pallas import tpu as pltpu

F32 = jnp.float32
BF16 = jnp.bfloat16
HIGHEST = lax.Precision.HIGHEST

D_MODEL = 2048
GRID_W = 64
EPS = 1e-6
NEG_INF = -1e30
D_INNER = 2 * D_MODEL
SSM_HEADDIM = 64
SSM_HEADS = D_INNER // SSM_HEADDIM
SSM_GROUPS = 8
HEADS_PER_GROUP = SSM_HEADS // SSM_GROUPS
GROUP_W = HEADS_PER_GROUP * SSM_HEADDIM
D_STATE = 128
D_CONV = 5
CHUNK = 128
HEAD_DIM = 128
ATTN_HEADS = D_MODEL // HEAD_DIM
KV_HEADS = 4
Q_PER_KV = ATTN_HEADS // KV_HEADS
ATTN_BLOCK = 128
ROPE_BASE = 10000.0
ROPE_PAIRS = HEAD_DIM // 4
PEER_HEADS = 8
N_KEYS = 128
N_EXPERTS = N_KEYS * N_KEYS
KEY_HALF = 128
PEER_TOPK = 16
K_W = KV_HEADS * HEAD_DIM
GN_W = SSM_GROUPS * D_STATE
XBC_W = D_INNER + 2 * GN_W
DT_W = 2 * SSM_HEADS
Q_W = ATTN_HEADS * HEAD_DIM
Z_W = D_INNER
GATE_W = 2 * D_MODEL
CTX_COLS = 2 * K_W + XBC_W + DT_W

VMEM_LIMIT_BYTES = 52 * 1024 * 1024
VMEM_LIMIT_PEER_BYTES = 58 * 1024 * 1024


def _params(*sem, vmem=VMEM_LIMIT_BYTES):
    return pltpu.CompilerParams(dimension_semantics=sem, vmem_limit_bytes=vmem)


def _ada_kernel(c_ref, w_ref, b_ref, o_ref):
    c = c_ref[...]
    s = c * jax.nn.sigmoid(c)
    o_ref[...] = jnp.dot(s.astype(BF16), w_ref[...].astype(BF16),
                         preferred_element_type=F32) + b_ref[...]


def _ada(cc, w, b):
    rows, d = cc.shape
    n = w.shape[1]
    tn = 1024
    return pl.pallas_call(
        _ada_kernel,
        grid=(n // tn,),
        in_specs=[pl.BlockSpec((rows, d), lambda j: (0, 0)),
                  pl.BlockSpec((d, tn), lambda j: (0, j)),
                  pl.BlockSpec((1, tn), lambda j: (0, j))],
        out_specs=pl.BlockSpec((rows, tn), lambda j: (0, j)),
        out_shape=jax.ShapeDtypeStruct((rows, n), F32),
        compiler_params=_params("arbitrary"),
        name="ada_mod",
    )(cc, w, b)


def _norm_mod_kernel(x_ref, g_ref, sc_ref, sh_ref, o_ref):
    x = x_ref[0]
    ms = jnp.mean(x * x, axis=-1, keepdims=True)
    xn = x * lax.rsqrt(ms + EPS)
    u = (xn * g_ref[...]) * (1.0 + sc_ref[0]) + sh_ref[0]
    o_ref[0] = u.astype(BF16)


def _norm_mod(x, g, scale, shift):
    b, s, d = x.shape
    ts = min(256, s)
    return pl.pallas_call(
        _norm_mod_kernel,
        grid=(b, s // ts),
        in_specs=[pl.BlockSpec((1, ts, d), lambda i, j: (i, j, 0)),
                  pl.BlockSpec((1, d), lambda i, j: (0, 0)),
                  pl.BlockSpec((1, 1, d), lambda i, j: (i, 0, 0)),
                  pl.BlockSpec((1, 1, d), lambda i, j: (i, 0, 0))],
        out_specs=pl.BlockSpec((1, ts, d), lambda i, j: (i, j, 0)),
        out_shape=jax.ShapeDtypeStruct((b, s, d), BF16),
        compiler_params=_params("arbitrary", "arbitrary"),
        name="norm_mod",
    )(x, g.reshape(1, d), scale, shift)


def _mm_kernel(a_ref, w_ref, o_ref):
    o_ref[...] = jnp.dot(a_ref[...], w_ref[...], preferred_element_type=F32).astype(o_ref.dtype)


def _mm_rope_kernel(a_ref, w_ref, cos_ref, sa_ref, sb_ref, o_ref):
    acc = jnp.dot(a_ref[...], w_ref[...], preferred_element_type=F32)
    cos, sa, sb = cos_ref[...], sa_ref[...], sb_ref[...]
    for j in range(acc.shape[1] // HEAD_DIM):
        t = acc[:, j * HEAD_DIM:(j + 1) * HEAD_DIM]
        r = t * cos + pltpu.roll(t, HEAD_DIM - ROPE_PAIRS, 1) * sa + pltpu.roll(t, ROPE_PAIRS, 1) * sb
        o_ref[:, j * HEAD_DIM:(j + 1) * HEAD_DIM] = r.astype(o_ref.dtype)


def _mm(a, w, out_dtype, rope=None, seq=None):
    m, k = a.shape
    n = w.shape[1]
    tm = min(1024, m if seq is None else seq)
    tn = 1024 if n % 1024 == 0 else min(512, n)
    in_specs = [pl.BlockSpec((tm, k), lambda i, j: (i, 0)),
                pl.BlockSpec((k, tn), lambda i, j: (0, j))]
    args = [a, w]
    kern = _mm_kernel
    if rope is not None:
        per = seq // tm
        in_specs += [pl.BlockSpec((tm, HEAD_DIM), lambda i, j: (i % per, 0))] * 3
        args += list(rope)
        kern = _mm_rope_kernel
    return pl.pallas_call(
        kern,
        grid=(m // tm, n // tn),
        in_specs=in_specs,
        out_specs=pl.BlockSpec((tm, tn), lambda i, j: (i, j)),
        out_shape=jax.ShapeDtypeStruct((m, n), out_dtype),
        compiler_params=_params("arbitrary", "arbitrary"),
        name="in_proj_rope" if rope is not None else "in_proj",
    )(*args)


ATTN_HEADS_PER_STEP = 4


def _attn_kernel(sink_ref, q_ref, kp_ref, kc_ref, kn_ref, vp_ref, vc_ref, vn_ref, kx_ref, vx_ref, o_ref):
    hg = pl.program_id(1)
    n = pl.program_id(2)
    nb = pl.num_programs(2)
    blk = ATTN_BLOCK
    nq = Q_PER_KV * blk
    nk = 3 * blk + kx_ref.shape[1]
    qi = lax.broadcasted_iota(jnp.int32, (nq, nk), 0) & (blk - 1)
    kj = lax.broadcasted_iota(jnp.int32, (nq, nk), 1)
    off_prev = jnp.where(n > 0, 0, 4 * blk)
    off_next = jnp.where(n < nb - 1, 0, 4 * blk)
    bad_prev = (kj < blk) & (kj < qi + off_prev)
    bad_next = (kj >= 2 * blk) & (kj < 3 * blk) & (kj - 2 * blk > qi - off_next)
    bad = bad_prev | bad_next
    rr = lax.broadcasted_iota(jnp.int32, (nq, 1), 0) // blk
    heads = range(ATTN_HEADS_PER_STEP)

    def head_cols(ref, i):
        return ref[0, :, i * HEAD_DIM:(i + 1) * HEAD_DIM]

    scores = []
    for i in heads:
        q = q_ref[0, :, i * Q_PER_KV * HEAD_DIM:(i + 1) * Q_PER_KV * HEAD_DIM]
        qs = jnp.concatenate([q[:, r * HEAD_DIM:(r + 1) * HEAD_DIM] for r in range(Q_PER_KV)], axis=0)
        kcat = jnp.concatenate([head_cols(kp_ref, i), head_cols(kc_ref, i), head_cols(kn_ref, i),
                                head_cols(kx_ref, i)], axis=0)
        scores.append(lax.dot_general(qs, kcat, (((1,), (1,)), ((), ())), preferred_element_type=F32))
    probs = []
    for i in heads:
        s = jnp.where(bad, NEG_INF, scores[i] * (HEAD_DIM ** -0.5))
        sk = jnp.zeros((nq, 1), F32)
        for r in range(Q_PER_KV):
            sk = jnp.where(rr == r, sink_ref[(hg * ATTN_HEADS_PER_STEP + i) * Q_PER_KV + r], sk)
        mx = jnp.maximum(jnp.max(s, axis=1, keepdims=True), sk)
        e = jnp.exp(s - mx)
        den = jnp.sum(e, axis=1, keepdims=True) + jnp.exp(sk - mx)
        probs.append((e / den).astype(BF16))
    for i in heads:
        vcat = jnp.concatenate([head_cols(vp_ref, i), head_cols(vc_ref, i), head_cols(vn_ref, i),
                                head_cols(vx_ref, i)], axis=0)
        o = jnp.dot(probs[i], vcat, preferred_element_type=F32)
        for r in range(Q_PER_KV):
            c0 = (i * Q_PER_KV + r) * HEAD_DIM
            o_ref[0, :, c0:c0 + HEAD_DIM] = o[r * blk:(r + 1) * blk, :].astype(o_ref.dtype)


def _attention(q, k, v, kc, vc, sink):
    b, s, _ = q.shape
    c = kc.shape[1]
    nb = s // ATTN_BLOCK
    qw = ATTN_HEADS_PER_STEP * Q_PER_KV * HEAD_DIM
    kw = ATTN_HEADS_PER_STEP * HEAD_DIM
    kv_spec = lambda f: pl.BlockSpec((1, ATTN_BLOCK, kw), f)
    prev = lambda bi, h, n: (bi, jnp.maximum(n - 1, 0), h)
    cur = lambda bi, h, n: (bi, n, h)
    nxt = lambda bi, h, n: (bi, jnp.minimum(n + 1, nb - 1), h)
    ctx_spec = pl.BlockSpec((1, c, kw), lambda bi, h, n: (bi, 0, h))
    return pl.pallas_call(
        _attn_kernel,
        grid=(b, KV_HEADS // ATTN_HEADS_PER_STEP, nb),
        in_specs=[pl.BlockSpec(memory_space=pltpu.SMEM),
                  pl.BlockSpec((1, ATTN_BLOCK, qw), lambda bi, h, n: (bi, n, h)),
                  kv_spec(prev), kv_spec(cur), kv_spec(nxt),
                  kv_spec(prev), kv_spec(cur), kv_spec(nxt),
                  ctx_spec, ctx_spec],
        out_specs=pl.BlockSpec((1, ATTN_BLOCK, qw), lambda bi, h, n: (bi, n, h)),
        out_shape=jax.ShapeDtypeStruct((b, s, Q_W), BF16),
        compiler_params=_params("arbitrary", "arbitrary", "arbitrary"),
        name="window_attn",
    )(sink, q, k, k, k, v, v, v, kc, vc)


CONV_HALO = 8
CONV_LANES = 512


def _conv_kernel(prev_ref, cur_ref, next_ref, w_ref, b_ref, *out_refs, widths):
    r = pl.program_id(1)
    rows = cur_ref.shape[1]
    keep_prev = jnp.where(r > 0, 1.0, 0.0)
    keep_next = jnp.where(r < pl.num_programs(1) - 1, 1.0, 0.0)
    col = 0
    for o_ref, width in zip(out_refs, widths):
        for c0 in range(0, width, CONV_LANES):
            lanes = slice(col + c0, col + c0 + CONV_LANES)
            xw = jnp.concatenate([prev_ref[0, :, lanes] * keep_prev, cur_ref[0, :, lanes],
                                  next_ref[0, :, lanes] * keep_next], axis=0)
            acc = b_ref[:, lanes]
            for j in range(D_CONV):
                off = j - D_CONV // 2
                tap = xw if off == 0 else pltpu.roll(xw, (-off) % xw.shape[0], 0)
                acc = acc + tap[CONV_HALO:CONV_HALO + rows, :] * w_ref[j:j + 1, lanes]
            o_ref[0, :, c0:c0 + CONV_LANES] = (acc * jax.nn.sigmoid(acc)).astype(o_ref.dtype)
        col += width


def _conv_silu(xbc, w, bias, widths):
    b, l, c = xbc.shape
    rows = min(256, l)
    per = rows // CONV_HALO
    last = l // CONV_HALO - 1
    return pl.pallas_call(
        functools.partial(_conv_kernel, widths=widths),
        grid=(b, l // rows),
        in_specs=[pl.BlockSpec((1, CONV_HALO, c), lambda i, r: (i, jnp.maximum(r * per - 1, 0), 0)),
                  pl.BlockSpec((1, rows, c), lambda i, r: (i, r, 0)),
                  pl.BlockSpec((1, CONV_HALO, c), lambda i, r: (i, jnp.minimum((r + 1) * per, last), 0)),
                  pl.BlockSpec((D_CONV, c), lambda i, r: (0, 0)),
                  pl.BlockSpec((1, c), lambda i, r: (0, 0))],
        out_specs=[pl.BlockSpec((1, rows, wd), lambda i, r: (i, r, 0)) for wd in widths],
        out_shape=[jax.ShapeDtypeStruct((b, l, wd), F32) for wd in widths],
        compiler_params=_params("arbitrary", "arbitrary"),
        name="dwconv_silu",
    )(xbc, xbc, xbc, w, bias)


def _dt_kernel(dt_ref, bias_ref, alog_ref, dto_ref, acs_ref):
    raw = dt_ref[0] + bias_ref[...]
    dt = jnp.maximum(raw, 0.0) + jnp.log1p(jnp.exp(-jnp.abs(raw)))
    a = dt * (-jnp.exp(alog_ref[...]))
    li = lax.broadcasted_iota(jnp.int32, (CHUNK, CHUNK), 0)
    si = lax.broadcasted_iota(jnp.int32, (CHUNK, CHUNK), 1)
    lower = jnp.where(li >= si, 1.0, 0.0).astype(F32)
    upper = jnp.where(li <= si, 1.0, 0.0).astype(F32)
    pre = jnp.dot(lower, a, preferred_element_type=F32, precision=HIGHEST)
    suf = jnp.dot(upper, a, preferred_element_type=F32, precision=HIGHEST)
    lane = lax.broadcasted_iota(jnp.int32, (1, DT_W), 1)
    dto_ref[0] = dt
    acs_ref[0] = jnp.where(lane < SSM_HEADS, pre, suf)


def _dt_prep(dt_raw, dt_bias, a_log):
    b, l, _ = dt_raw.shape
    spec = pl.BlockSpec((1, CHUNK, DT_W), lambda i, c: (i, c, 0))
    vec = pl.BlockSpec((1, DT_W), lambda i, c: (0, 0))
    return pl.pallas_call(
        _dt_kernel,
        grid=(b, l // CHUNK),
        in_specs=[spec, vec, vec],
        out_specs=[spec, spec],
        out_shape=[jax.ShapeDtypeStruct((b, l, DT_W), F32)] * 2,
        compiler_params=_params("arbitrary", "arbitrary"),
        name="dt_prep",
    )(dt_raw, dt_bias.reshape(1, DT_W), a_log.reshape(1, DT_W))


F32_PIECES = 3
EXPAND_K = F32_PIECES * HEADS_PER_GROUP
GROUPS_PER_TRIP = 2


def _group_layouts(t, direction):
    b, l, _ = t.shape
    t = t[..., direction * SSM_HEADS:(direction + 1) * SSM_HEADS].reshape(b, l, SSM_GROUPS, HEADS_PER_GROUP)
    cols = jnp.tile(jnp.transpose(t, (0, 2, 1, 3)), (1, 1, 1, F32_PIECES))
    return cols, jnp.transpose(t, (0, 2, 3, 1))


def _expand_heads(v3, width):
    lane = lax.broadcasted_iota(jnp.int32, (1, EXPAND_K), 1)
    hi = v3.astype(BF16).astype(F32)
    rest = v3 - hi
    mid = rest.astype(BF16).astype(F32)
    pieces = jnp.where(lane < HEADS_PER_GROUP, hi,
                       jnp.where(lane < 2 * HEADS_PER_GROUP, mid, rest - mid)).astype(BF16)
    shift = width.bit_length() - 1
    r = lax.broadcasted_iota(jnp.int32, (EXPAND_K, HEADS_PER_GROUP * width), 0)
    c = lax.broadcasted_iota(jnp.int32, (EXPAND_K, HEADS_PER_GROUP * width), 1)
    spread = jnp.where((r & (HEADS_PER_GROUP - 1)) == (c >> shift), 1.0, 0.0).astype(BF16)
    return jnp.dot(pieces, spread, preferred_element_type=F32)


def _ssd_kernel(*refs, reverse, emit_y, add_prev):
    it = iter(refs)
    x_ref, b_ref = next(it), next(it)
    c_ref = next(it) if emit_y else None
    dtc_ref, acsc_ref = next(it), next(it)
    acsr_ref = next(it) if emit_y else None
    h0_ref = next(it)
    if add_prev:
        yp_ref, dexp_ref, z_ref, gn_ref = next(it), next(it), next(it), next(it)
    y_ref = next(it) if emit_y else None
    hf_ref = next(it)
    hs_ref = next(it)
    ybuf_ref = next(it) if add_prev else None

    @pl.when(pl.program_id(1) == 0)
    def _():
        hs_ref[...] = h0_ref[0]

    edge = 0 if reverse else CHUNK - 1
    li = lax.broadcasted_iota(jnp.int32, (CHUNK, CHUNK), 0)
    si = lax.broadcasted_iota(jnp.int32, (CHUNK, CHUNK), 1)
    causal = (li <= si) if reverse else (li >= si)
    lo = lax.broadcasted_iota(jnp.int32, (1, 2 * SSM_HEADDIM), 1) < SSM_HEADDIM

    def load(g):
        lanes = pl.ds(pl.multiple_of(g * GROUP_W, GROUP_W), GROUP_W)
        nl = pl.ds(pl.multiple_of(g * D_STATE, D_STATE), D_STATE)
        v = dict(g=g, lanes=lanes, nl=nl, xg=x_ref[0, :, lanes], bmat=b_ref[0, :, nl],
                 acs3=acsc_ref[0, g], dt3=dtc_ref[0, g], hs=hs_ref[g])
        if emit_y:
            v.update(cmat=c_ref[0, :, nl], acs_r=acsr_ref[0, g])
        if add_prev:
            v.update(yp=yp_ref[0, :, lanes], dexp=dexp_ref[:, lanes])
        return v

    def stage1(v):
        acs3 = v["acs3"]
        last3 = acs3[edge:edge + 1, :]
        v["eacs_x"] = _expand_heads(jnp.exp(acs3), SSM_HEADDIM)
        v["dt_x"] = _expand_heads(v["dt3"], SSM_HEADDIM)
        v["wend_x"] = _expand_heads(jnp.exp(last3 - acs3), SSM_HEADDIM)
        if emit_y:
            c16 = v["cmat"].astype(BF16)
            v["col_x"] = _expand_heads(acs3, CHUNK)
            v["cbm"] = lax.dot_general(c16, v["bmat"].astype(BF16), (((1,), (1,)), ((), ())),
                                       preferred_element_type=F32)
            v["y_off"] = jnp.dot(c16, v["hs"].astype(BF16), preferred_element_type=F32)

    def stage2(v):
        g = v["g"]
        wx = v["xg"] * v["dt_x"]
        wxe = wx * v["wend_x"]
        states = jnp.dot(v["bmat"].T.astype(BF16), wxe.astype(BF16), preferred_element_type=F32)
        hs_new = v["eacs_x"][edge:edge + 1, :] * v["hs"] + states
        hs_ref[g] = hs_new
        hf_ref[0, g] = hs_new
        v["wx16"] = wx.astype(BF16)

    def stage3(v):
        parts = []
        for i in range(HEADS_PER_GROUP // 2):
            ms = []
            for j in (2 * i, 2 * i + 1):
                seg = v["col_x"][:, j * CHUNK:(j + 1) * CHUNK] - v["acs_r"][j:j + 1, :]
                dec = jnp.exp(jnp.where(causal, seg, -jnp.inf))
                ms.append((v["cbm"] * dec).astype(BF16))
            wp = v["wx16"][:, i * 128:(i + 1) * 128]
            zero = jnp.zeros_like(wp)
            rhs = jnp.concatenate([jnp.where(lo, wp, zero), jnp.where(lo, zero, wp)], axis=0)
            parts.append(jnp.dot(jnp.concatenate(ms, axis=1), rhs, preferred_element_type=F32))
        y = jnp.concatenate(parts, axis=1) + v["y_off"] * v["eacs_x"]
        if add_prev:
            ybuf_ref[:, v["lanes"]] = (v["yp"] + y) + v["dexp"] * v["xg"]
        else:
            y_ref[0, :, v["lanes"]] = y

    def group_batch(p, carry):
        vs = [load(GROUPS_PER_TRIP * p + k) for k in range(GROUPS_PER_TRIP)]
        for stage in (stage1, stage2) + ((stage3,) if emit_y else ()):
            for v in vs:
                stage(v)
        return carry

    lax.fori_loop(0, SSM_GROUPS // GROUPS_PER_TRIP, group_batch, 0)

    if add_prev:
        z = z_ref[0]
        t = ybuf_ref[...] * (z * jax.nn.sigmoid(z))
        ms = jnp.mean(t * t, axis=-1, keepdims=True)
        y_ref[0] = ((t * lax.rsqrt(ms + EPS)) * gn_ref[...]).astype(y_ref.dtype)


def _ssd_pass(x, bm, cm, dt, acs, h0, direction, emit_y, y_prev=None, d_exp=None, z=None, g_norm=None):
    b, l, _ = x.shape
    nc = l // CHUNK
    reverse = direction == 1
    add_prev = y_prev is not None
    dtc, _ = _group_layouts(dt, direction)
    acsc, acsr = _group_layouts(acs, direction)
    cidx = (lambda c: nc - 1 - c) if reverse else (lambda c: c)
    xspec = pl.BlockSpec((1, CHUNK, D_INNER), lambda i, c: (i, cidx(c), 0))
    bspec = pl.BlockSpec((1, CHUNK, GN_W), lambda i, c: (i, cidx(c), 0))
    colspec = pl.BlockSpec((1, SSM_GROUPS, CHUNK, EXPAND_K), lambda i, c: (i, 0, cidx(c), 0))
    rowspec = pl.BlockSpec((1, SSM_GROUPS, HEADS_PER_GROUP, CHUNK), lambda i, c: (i, 0, 0, cidx(c)))
    hspec = pl.BlockSpec((1, SSM_GROUPS, D_STATE, GROUP_W), lambda i, c: (i, 0, 0, 0))
    in_specs, args = [xspec, bspec], [x, bm]
    if emit_y:
        in_specs.append(bspec)
        args.append(cm)
    in_specs += [colspec, colspec]
    args += [dtc, acsc]
    if emit_y:
        in_specs.append(rowspec)
        args.append(acsr)
    in_specs.append(hspec)
    args.append(h0)
    scratch = [pltpu.VMEM((SSM_GROUPS, D_STATE, GROUP_W), F32)]
    if add_prev:
        vec = pl.BlockSpec((1, D_INNER), lambda i, c: (0, 0))
        in_specs += [xspec, vec, xspec, vec]
        args += [y_prev, d_exp, z, g_norm.reshape(1, D_INNER)]
        scratch.append(pltpu.VMEM((CHUNK, D_INNER), F32))
    out_specs, out_shape = [], []
    if emit_y:
        out_specs.append(xspec)
        out_shape.append(jax.ShapeDtypeStruct((b, l, D_INNER), BF16 if add_prev else F32))
    out_specs.append(hspec)
    out_shape.append(jax.ShapeDtypeStruct((b, SSM_GROUPS, D_STATE, GROUP_W), F32))
    res = pl.pallas_call(
        functools.partial(_ssd_kernel, reverse=reverse, emit_y=emit_y, add_prev=add_prev),
        grid=(b, nc),
        in_specs=in_specs,
        out_specs=out_specs,
        out_shape=out_shape,
        scratch_shapes=scratch,
        compiler_params=_params("arbitrary", "arbitrary"),
        name="ssd_scan",
    )(*args)
    return (res[0], res[1]) if emit_y else (None, res[0])


def _merge_kernel(yn_ref, at_ref, ws_ref, wa_ref, g1_ref, g2_ref, o_ref):
    t1 = jnp.dot(yn_ref[...], ws_ref[...], preferred_element_type=F32)
    t2 = jnp.dot(at_ref[...], wa_ref[...], preferred_element_type=F32)
    o_ref[...] = (jax.nn.sigmoid(g1_ref[...]) * t1 + jax.nn.sigmoid(g2_ref[...]) * t2).astype(o_ref.dtype)


def _merge(yn, attn, w_ssm, w_attn, gates):
    m = yn.shape[0]
    tm = min(512, m)
    tn = 1024
    nj = D_MODEL // tn
    return pl.pallas_call(
        _merge_kernel,
        grid=(m // tm, nj),
        in_specs=[pl.BlockSpec((tm, D_INNER), lambda i, j: (i, 0)),
                  pl.BlockSpec((tm, Q_W), lambda i, j: (i, 0)),
                  pl.BlockSpec((D_INNER, tn), lambda i, j: (0, j)),
                  pl.BlockSpec((Q_W, tn), lambda i, j: (0, j)),
                  pl.BlockSpec((tm, tn), lambda i, j: (i, j)),
                  pl.BlockSpec((tm, tn), lambda i, j: (i, nj + j))],
        out_specs=pl.BlockSpec((tm, tn), lambda i, j: (i, j)),
        out_shape=jax.ShapeDtypeStruct((m, D_MODEL), BF16),
        compiler_params=_params("arbitrary", "arbitrary"),
        name="branch_merge",
    )(yn, attn, w_ssm, w_attn, gates, gates)


def _out_proj_kernel(a_ref, w_ref, x_ref, g_ref, o_ref):
    mix = jnp.dot(a_ref[0], w_ref[...], preferred_element_type=F32)
    o_ref[0] = x_ref[0] + g_ref[0] * mix


def _out_proj(merged, w_out, x, gate):
    b, s, d = x.shape
    tm = min(1024, s)
    tn = 1024
    return pl.pallas_call(
        _out_proj_kernel,
        grid=(b, s // tm, d // tn),
        in_specs=[pl.BlockSpec((1, tm, d), lambda i, r, j: (i, r, 0)),
                  pl.BlockSpec((d, tn), lambda i, r, j: (0, j)),
                  pl.BlockSpec((1, tm, tn), lambda i, r, j: (i, r, j)),
                  pl.BlockSpec((1, 1, tn), lambda i, r, j: (i, 0, j))],
        out_specs=pl.BlockSpec((1, tm, tn), lambda i, r, j: (i, r, j)),
        out_shape=jax.ShapeDtypeStruct((b, s, d), F32),
        compiler_params=_params("arbitrary", "arbitrary", "arbitrary"),
        name="out_proj",
    )(merged, w_out, x, gate)


PEER_TT = 256


def _topk_rounds(s, n_rounds, break_ties):
    rows = s.shape[0]
    rid = lax.broadcasted_iota(jnp.int32, s.shape, 0).astype(F32)
    rank = jnp.full(s.shape, float(n_rounds), F32)
    vals = []
    for r in range(n_rounds):
        m = jnp.max(s, axis=0, keepdims=True)
        hit = s == m
        if break_ties:
            hit = rid == jnp.min(jnp.where(hit, rid, float(rows)), axis=0, keepdims=True)
        rank = jnp.where(hit, float(r), rank)
        s = jnp.where(hit, -jnp.inf, s)
        vals.append(m)
    return jnp.concatenate(vals, axis=0), rank


def _topk_rows(scores, n_rounds, scratch):
    worst = None
    for s, (v_scr, r_scr) in zip(scores, scratch):
        v_scr[...], r_scr[...] = _topk_rounds(s, n_rounds, False)
        taken = jnp.sum(jnp.where(r_scr[...] < float(n_rounds), 1.0, 0.0), axis=0, keepdims=True)
        worst = taken if worst is None else jnp.maximum(worst, taken)

    @pl.when(jnp.max(worst) > float(n_rounds))
    def _():
        for s, (v_scr, r_scr) in zip(scores, scratch):
            v_scr[...], r_scr[...] = _topk_rounds(s, n_rounds, True)

    return [(v_scr[...], r_scr[...]) for v_scr, r_scr in scratch]


def _staircase_counts(v1, v2):
    k = v1.shape[0]
    aid = lax.broadcasted_iota(jnp.int32, v1.shape, 0).astype(F32)
    n = jnp.zeros_like(v1)
    front = v1 + v2[0:1, :]
    picked = []
    for _ in range(k):
        m = jnp.max(front, axis=0, keepdims=True)
        idx = jnp.min(jnp.where(front == m, aid, float(k)), axis=0, keepdims=True)
        hit = aid == idx
        n = jnp.where(hit, n + 1.0, n)
        nxt = jnp.full_like(v1, -jnp.inf)
        for b in range(1, k):
            nxt = jnp.where(n == float(b), v2[b:b + 1, :], nxt)
        front = jnp.where(hit, v1 + nxt, front)
        picked.append(m)
    return n, jnp.concatenate(picked, axis=0)


PEER_HEADS_PER_TRIP = 4


def _peer_topk_kernel(u_ref, wq_ref, k1_ref, k2_ref, e1_ref, nc_ref, e2_ref, rb_ref, q_scr, *vr_scr):
    q_scr[...] = jnp.dot(u_ref[...], wq_ref[...], preferred_element_type=F32)
    scratch = tuple((vr_scr[2 * i], vr_scr[2 * i + 1]) for i in range(len(vr_scr) // 2))

    def scores(h):
        base = pl.multiple_of(h * 2 * KEY_HALF, 2 * KEY_HALF)
        q1 = q_scr[:, pl.ds(base, KEY_HALF)].astype(BF16)
        q2 = q_scr[:, pl.ds(pl.multiple_of(base + KEY_HALF, KEY_HALF), KEY_HALF)].astype(BF16)
        nt = (((1,), (1,)), ((), ()))
        return (lax.dot_general(k1_ref[...], q1, nt, preferred_element_type=F32),
                lax.dot_general(k2_ref[...], q2, nt, preferred_element_type=F32))

    def trip(p, carry):
        hs = [p * PEER_HEADS_PER_TRIP + i for i in range(PEER_HEADS_PER_TRIP)]
        sc = [scores(h) for h in hs]
        picks = _topk_rows([s for pair in sc for s in pair], PEER_TOPK, scratch)
        for i, h in enumerate(hs):
            finish(h, sc[i][0], sc[i][1], *picks[2 * i], *picks[2 * i + 1])
        return carry

    def finish(h, sc1, sc2, v1, r1, v2, r2):
        n, top = _staircase_counts(v1, v2)
        z = jnp.sum(jnp.exp(top - top[0:1, :]), axis=0, keepdims=True)
        nc = jnp.zeros_like(sc1)
        for a in range(PEER_TOPK):
            nc = jnp.where(r1 == float(a), n[a:a + 1, :], nc)
        e1_ref[h] = jnp.where(r1 < float(PEER_TOPK), jnp.exp(sc1 - v1[0:1, :]) / z, 0.0)
        nc_ref[h] = nc
        e2_ref[h] = jnp.exp(sc2 - v2[0:1, :]).astype(e2_ref.dtype)
        rb_ref[h] = r2.astype(rb_ref.dtype)

    lax.fori_loop(0, PEER_HEADS // PEER_HEADS_PER_TRIP, trip, 0)


def _peer_topk(u2, wq, keys1, keys2):
    m, d = u2.shape
    tt = min(PEER_TT, m)
    big = pl.BlockSpec((PEER_HEADS, N_KEYS, tt), lambda i: (0, 0, i))
    bshape = jax.ShapeDtypeStruct((PEER_HEADS, N_KEYS, m), F32)
    bshape16 = jax.ShapeDtypeStruct((PEER_HEADS, N_KEYS, m), BF16)
    kspec = pl.BlockSpec((N_KEYS, KEY_HALF), lambda i: (0, 0))
    return pl.pallas_call(
        _peer_topk_kernel,
        grid=(m // tt,),
        in_specs=[pl.BlockSpec((tt, d), lambda i: (i, 0)),
                  pl.BlockSpec((d, PEER_HEADS * 2 * KEY_HALF), lambda i: (0, 0)),
                  kspec, kspec],
        out_specs=[big, big, big, big],
        out_shape=[bshape, bshape, bshape16, bshape16],
        scratch_shapes=[pltpu.VMEM((tt, PEER_HEADS * 2 * KEY_HALF), F32)]
        + [pltpu.VMEM((PEER_TOPK, tt), F32), pltpu.VMEM((N_KEYS, tt), F32)] * (2 * PEER_HEADS_PER_TRIP),
        compiler_params=_params("arbitrary"),
        name="peer_topk",
    )(u2, wq, keys1, keys2)


PEER_NI = 8
PEER_DENSE_TT = 512
PEER_LANES = 256
SQRT_HALF = float(math.sqrt(0.5))


def _peer_stage(tile, lanes, ut_ref, pu_ref, pvt_ref, e1_ref, nc_ref, e2_ref, rb_ref, acc_ref, ht_w, ht_r):
    ht_w[:, lanes] = lax.dot_general(pu_ref[...], ut_ref[lanes, :], (((1,), (1,)), ((), ())),
                                     preferred_element_type=F32)
    gas = []
    for ii in range(PEER_NI):
        i1 = tile * PEER_NI + ii
        hb = ht_r[ii * N_KEYS:(ii + 1) * N_KEYS, lanes]
        act = 0.5 * hb * (1.0 + lax.erf(hb * SQRT_HALF))
        g = jnp.zeros(hb.shape, BF16)
        for h in range(PEER_HEADS):
            e1c = e1_ref[h, pl.ds(i1, 1), lanes].astype(BF16)
            ncc = nc_ref[h, pl.ds(i1, 1), lanes].astype(BF16)
            g = g + jnp.where(rb_ref[h, :, lanes] < ncc, e1c * e2_ref[h, :, lanes], jnp.zeros((), BF16))
        gas.append(g * act.astype(BF16))
    acc_ref[:, lanes] += jnp.dot(pvt_ref[0], jnp.concatenate(gas, axis=0), preferred_element_type=F32)


def _peer_dense_kernel(ut_ref, pu_ref, pvt_ref, e1_ref, nc_ref, e2_ref, rb_ref,
                       h_ref, g5_ref, gf_ref, o_ref, acc_ref, ht0_ref, ht1_ref):
    e = pl.program_id(1)
    ne = pl.num_programs(1)
    part = pl.program_id(2)
    tile = jnp.maximum(e - 1, 0)

    for k in range(ut_ref.shape[0] // PEER_LANES):
        lanes = slice(k * PEER_LANES, (k + 1) * PEER_LANES)
        stage = functools.partial(_peer_stage, tile, lanes, ut_ref, pu_ref, pvt_ref, e1_ref, nc_ref, e2_ref,
                                  rb_ref, acc_ref)

        @pl.when((part == k) & (e == 0))
        def _():
            acc_ref[:, lanes] = jnp.zeros((acc_ref.shape[0], PEER_LANES), F32)
            ht1_ref[:, lanes] = jnp.zeros((ht1_ref.shape[0], PEER_LANES), F32)

        @pl.when((part == k) & (e % 2 == 0))
        def _():
            stage(ht0_ref, ht1_ref)

        @pl.when((part == k) & (e % 2 == 1))
        def _():
            stage(ht1_ref, ht0_ref)

        @pl.when((part == k) & (e == ne - 1))
        def _():
            hres = h_ref[0, lanes, :] + g5_ref[0] * acc_ref[:, lanes].T
            ms = jnp.mean(hres * hres, axis=-1, keepdims=True)
            o_ref[0, lanes, :] = (hres * lax.rsqrt(ms + EPS)) * gf_ref[...]


def _peer_dense(u2, pu, pvt, e1, nc, e2, rb, h1, gate5, gfinal):
    b, s, d = h1.shape
    m = b * s
    tt = min(PEER_DENSE_TT, s)
    per = s // tt
    te = PEER_NI * N_KEYS
    ne = N_EXPERTS // te
    once = pl.Buffered(1)
    big = pl.BlockSpec((PEER_HEADS, N_KEYS, tt), lambda i, e, k: (0, 0, i), pipeline_mode=once)
    return pl.pallas_call(
        _peer_dense_kernel,
        grid=(m // tt, ne + 1, tt // PEER_LANES),
        in_specs=[pl.BlockSpec((tt, d), lambda i, e, k: (i, 0), pipeline_mode=once),
                  pl.BlockSpec((te, d), lambda i, e, k: (jnp.minimum(e, ne - 1), 0)),
                  pl.BlockSpec((1, d, te), lambda i, e, k: (jnp.maximum(e - 1, 0), 0, 0)),
                  big, big, big, big,
                  pl.BlockSpec((1, tt, d), lambda i, e, k: (i // per, i % per, 0), pipeline_mode=once),
                  pl.BlockSpec((1, 1, d), lambda i, e, k: (i // per, 0, 0)),
                  pl.BlockSpec((1, d), lambda i, e, k: (0, 0))],
        out_specs=pl.BlockSpec((1, tt, d), lambda i, e, k: (i // per, i % per, 0)),
        out_shape=jax.ShapeDtypeStruct((b, s, d), F32),
        scratch_shapes=[pltpu.VMEM((d, tt), F32), pltpu.VMEM((te, tt), F32), pltpu.VMEM((te, tt), F32)],
        compiler_params=_params("arbitrary", "arbitrary", "arbitrary", vmem=VMEM_LIMIT_PEER_BYTES),
        name="peer_dense",
    )(u2, pu, pvt, e1, nc, e2, rb, h1, gate5, gfinal.reshape(1, d))


def _transpose_cast_kernel(w_ref, o_ref):
    half = w_ref.shape[0] // 2
    for k in range(2):
        o_ref[0, :, k * half:(k + 1) * half] = w_ref[k * half:(k + 1) * half, :].T.astype(o_ref.dtype)


def _transpose_cast(w, te):
    e, d = w.shape
    return pl.pallas_call(
        _transpose_cast_kernel,
        grid=(e // te,),
        in_specs=[pl.BlockSpec((te, d), lambda i: (i, 0))],
        out_specs=pl.BlockSpec((1, d, te), lambda i: (i, 0, 0)),
        out_shape=jax.ShapeDtypeStruct((e // te, d, te), BF16),
        compiler_params=_params("arbitrary"),
        name="transpose_cast",
    )(w)


def _rope_tables(rows):
    row = jnp.repeat(jnp.arange(rows), GRID_W).astype(F32)
    col = jnp.tile(jnp.arange(GRID_W), rows).astype(F32)
    freqs = ROPE_BASE ** (-jnp.arange(ROPE_PAIRS, dtype=F32) / ROPE_PAIRS)
    ar = row[:, None] * freqs
    ac = col[:, None] * freqs
    ang = jnp.concatenate([ar, ar, ac, ac], axis=-1)
    cos, sin = jnp.cos(ang), jnp.sin(ang)
    lane = jnp.arange(HEAD_DIM)
    first = (lane % (2 * ROPE_PAIRS)) < ROPE_PAIRS
    return cos, jnp.where(first, -sin, 0.0), jnp.where(first, 0.0, sin)


def kernel(x, c, ctx, c_ctx, ada_w, ada_b, norm1_g, w_in, conv_w, conv_b, dt_bias, a_log, ssm_d, ssm_norm_g,
           attn_sink, w_branch_ssm, w_branch_attn, w_out, norm2_g, peer_wq, peer_keys1, peer_keys2, peer_u,
           peer_v, final_norm_g):
    b, s, d = x.shape
    cl = ctx.shape[1]
    m = b * s
    l = 0

    cc = jnp.concatenate([c, c_ctx[None, :], jnp.zeros((8 - b - 1, d), F32)], axis=0)
    mod_all = _ada(cc, ada_w[l], ada_b[l][None, :])
    mod = mod_all[:b].reshape(b, 6, 1, d)
    modc = jnp.broadcast_to(mod_all[b].reshape(1, 6, 1, d), (b, 6, 1, d))

    u = _norm_mod(x, norm1_g[l], mod[:, 1], mod[:, 0]).reshape(m, d)
    uc = _norm_mod(ctx, norm1_g[l], modc[:, 1], modc[:, 0]).reshape(b * cl, d)

    wi = w_in[l]
    o = 0
    w_k = wi[:, o:o + K_W].astype(BF16); o += K_W
    w_v = wi[:, o:o + K_W].astype(BF16); o += K_W
    w_xbc = wi[:, o:o + XBC_W].astype(BF16); o += XBC_W
    w_dt = wi[:, o:o + DT_W].astype(BF16); o += DT_W
    w_q = wi[:, o:o + Q_W].astype(BF16); o += Q_W
    w_z = wi[:, o:o + Z_W].astype(BF16); o += Z_W
    w_g = wi[:, o:o + GATE_W].astype(BF16)

    rope = _rope_tables(s // GRID_W)
    q = _mm(u, w_q, BF16, rope=rope, seq=s).reshape(b, s, Q_W)
    k = _mm(u, w_k, BF16, rope=rope, seq=s).reshape(b, s, K_W)
    v = _mm(u, w_v, BF16).reshape(b, s, K_W)
    xbc = _mm(u, w_xbc, F32).reshape(b, s, XBC_W)
    dt_raw = _mm(u, w_dt, F32).reshape(b, s, DT_W)
    z = _mm(u, w_z, F32)
    gates = _mm(u, w_g, F32)
    kc = _mm(uc, w_k, BF16).reshape(b, cl, K_W)
    vc = _mm(uc, w_v, BF16).reshape(b, cl, K_W)
    xbcc = _mm(uc, w_xbc, F32).reshape(b, cl, XBC_W)
    dtc_raw = _mm(uc, w_dt, F32).reshape(b, cl, DT_W)

    attn = _attention(q, k, v, kc, vc, attn_sink[l]).reshape(m, Q_W)

    cw, cb = conv_w[l], conv_b[l][None, :]
    xs, bm, cm = _conv_silu(xbc, cw, cb, (D_INNER, GN_W, GN_W))
    xsc, bmc = _conv_silu(xbcc, cw, cb, (D_INNER, GN_W))
    dt, acs = _dt_prep(dt_raw, dt_bias[l], a_log[l])
    dtc, acsc = _dt_prep(dtc_raw, dt_bias[l], a_log[l])
    hzero = jnp.zeros((b, SSM_GROUPS, D_STATE, GROUP_W), F32)
    _, hcf = _ssd_pass(xsc, bmc, None, dtc, acsc, hzero, 0, False)
    _, hcb = _ssd_pass(xsc, bmc, None, dtc, acsc, hzero, 1, False)
    y_f, _ = _ssd_pass(xs, bm, cm, dt, acs, hcf, 0, True)
    d_exp = jnp.repeat(ssm_d[l], SSM_HEADDIM)[None, :]
    yn, _ = _ssd_pass(xs, bm, cm, dt, acs, hcb, 1, True, y_prev=y_f, d_exp=d_exp,
                      z=z.reshape(b, s, Z_W), g_norm=ssm_norm_g[l])
    yn = yn.reshape(m, D_INNER)
    merged = _merge(yn, attn, w_branch_ssm[l].astype(BF16), w_branch_attn[l].astype(BF16), gates)
    h1 = _out_proj(merged.reshape(b, s, d), w_out[l].astype(BF16), x, mod[:, 2])

    u2 = _norm_mod(h1, norm2_g[l], mod[:, 4], mod[:, 3]).reshape(m, d)
    e1, nc, e2, rb = _peer_topk(u2, peer_wq[l].astype(BF16),
                                peer_keys1[l].astype(BF16), peer_keys2[l].astype(BF16))
    pu = peer_u[l].astype(BF16)
    pvt = _transpose_cast(peer_v[l], PEER_NI * N_KEYS)
    return _peer_dense(u2, pu, pvt, e1, nc, e2, rb, h1, mod[:, 5], final_norm_g)
```

```python
import functools
import math

import jax
import jax.numpy as jnp
from jax import lax
from jax.experimental import pallas as pl
from jax.experimental.pallas import tpu as pltpu

F32 = jnp.float32
BF16 = jnp.bfloat16
HIGHEST = lax.Precision.HIGHEST

D_MODEL = 2048
GRID_W = 64
EPS = 1e-6
NEG_INF = -1e30
D_INNER = 2 * D_MODEL
SSM_HEADDIM = 64
SSM_HEADS = D_INNER // SSM_HEADDIM
SSM_GROUPS = 8
HEADS_PER_GROUP = SSM_HEADS // SSM_GROUPS
GROUP_W = HEADS_PER_GROUP * SSM_HEADDIM
D_STATE = 128
D_CONV = 5
CHUNK = 128
HEAD_DIM = 128
ATTN_HEADS = D_MODEL // HEAD_DIM
KV_HEADS = 4
Q_PER_KV = ATTN_HEADS // KV_HEADS
ATTN_BLOCK = 128
ROPE_BASE = 10000.0
ROPE_PAIRS = HEAD_DIM // 4
PEER_HEADS = 8
N_KEYS = 128
N_EXPERTS = N_KEYS * N_KEYS
KEY_HALF = 128
PEER_TOPK = 16
K_W = KV_HEADS * HEAD_DIM
GN_W = SSM_GROUPS * D_STATE
XBC_W = D_INNER + 2 * GN_W
DT_W = 2 * SSM_HEADS
Q_W = ATTN_HEADS * HEAD_DIM
Z_W = D_INNER
GATE_W = 2 * D_MODEL
CTX_COLS = 2 * K_W + XBC_W + DT_W

VMEM_LIMIT_BYTES = 52 * 1024 * 1024
VMEM_LIMIT_PEER_BYTES = 58 * 1024 * 1024


def _params(*sem, vmem=VMEM_LIMIT_BYTES):
    return pltpu.CompilerParams(dimension_semantics=sem, vmem_limit_bytes=vmem)


def _ada_kernel(c_ref, w_ref, b_ref, o_ref):
    c = c_ref[...]
    s = c * jax.nn.sigmoid(c)
    o_ref[...] = jnp.dot(s.astype(BF16), w_ref[...].astype(BF16),
                         preferred_element_type=F32) + b_ref[...]


def _ada(cc, w, b):
    rows, d = cc.shape
    n = w.shape[1]
    tn = 1024
    return pl.pallas_call(
        _ada_kernel,
        grid=(n // tn,),
        in_specs=[pl.BlockSpec((rows, d), lambda j: (0, 0)),
                  pl.BlockSpec((d, tn), lambda j: (0, j)),
                  pl.BlockSpec((1, tn), lambda j: (0, j))],
        out_specs=pl.BlockSpec((rows, tn), lambda j: (0, j)),
        out_shape=jax.ShapeDtypeStruct((rows, n), F32),
        compiler_params=_params("arbitrary"),
        name="ada_mod",
    )(cc, w, b)


def _norm_mod_kernel(x_ref, g_ref, sc_ref, sh_ref, o_ref):
    x = x_ref[0]
    ms = jnp.mean(x * x, axis=-1, keepdims=True)
    xn = x * lax.rsqrt(ms + EPS)
    u = (xn * g_ref[...]) * (1.0 + sc_ref[0]) + sh_ref[0]
    o_ref[0] = u.astype(BF16)


def _norm_mod(x, g, scale, shift):
    b, s, d = x.shape
    ts = min(512, s)
    return pl.pallas_call(
        _norm_mod_kernel,
        grid=(b, s // ts),
        in_specs=[pl.BlockSpec((1, ts, d), lambda i, j: (i, j, 0)),
                  pl.BlockSpec((1, d), lambda i, j: (0, 0)),
                  pl.BlockSpec((1, 1, d), lambda i, j: (i, 0, 0)),
                  pl.BlockSpec((1, 1, d), lambda i, j: (i, 0, 0))],
        out_specs=pl.BlockSpec((1, ts, d), lambda i, j: (i, j, 0)),
        out_shape=jax.ShapeDtypeStruct((b, s, d), BF16),
        compiler_params=_params("arbitrary", "arbitrary"),
        name="norm_mod",
    )(x, g.reshape(1, d), scale, shift)


def _mm_kernel(a_ref, w_ref, o_ref):
    o_ref[...] = jnp.dot(a_ref[...], w_ref[...], preferred_element_type=F32).astype(o_ref.dtype)


def _mm_rope_kernel(a_ref, w_ref, cos_ref, sa_ref, sb_ref, o_ref):
    acc = jnp.dot(a_ref[...], w_ref[...], preferred_element_type=F32)
    cos, sa, sb = cos_ref[...], sa_ref[...], sb_ref[...]
    for j in range(acc.shape[1] // HEAD_DIM):
        t = acc[:, j * HEAD_DIM:(j + 1) * HEAD_DIM]
        r = t * cos + pltpu.roll(t, HEAD_DIM - ROPE_PAIRS, 1) * sa + pltpu.roll(t, ROPE_PAIRS, 1) * sb
        o_ref[:, j * HEAD_DIM:(j + 1) * HEAD_DIM] = r.astype(o_ref.dtype)


def _mm(a, w, out_dtype, rope=None, seq=None):
    m, k = a.shape
    n = w.shape[1]
    tm = min(1024, m if seq is None else seq)
    tn = 1024 if n % 1024 == 0 else min(512, n)
    in_specs = [pl.BlockSpec((tm, k), lambda i, j: (i, 0)),
                pl.BlockSpec((k, tn), lambda i, j: (0, j))]
    args = [a, w]
    kern = _mm_kernel
    if rope is not None:
        per = seq // tm
        in_specs += [pl.BlockSpec((tm, HEAD_DIM), lambda i, j: (i % per, 0))] * 3
        args += list(rope)
        kern = _mm_rope_kernel
    return pl.pallas_call(
        kern,
        grid=(m // tm, n // tn),
        in_specs=in_specs,
        out_specs=pl.BlockSpec((tm, tn), lambda i, j: (i, j)),
        out_shape=jax.ShapeDtypeStruct((m, n), out_dtype),
        compiler_params=_params("arbitrary", "arbitrary"),
        name="in_proj_rope" if rope is not None else "in_proj",
    )(*args)


ATTN_HEADS_PER_STEP = 4


def _attn_kernel(sink_ref, q_ref, kp_ref, kc_ref, kn_ref, vp_ref, vc_ref, vn_ref, kx_ref, vx_ref, o_ref):
    hg = pl.program_id(1)
    n = pl.program_id(2)
    nb = pl.num_programs(2)
    blk = ATTN_BLOCK
    nq = Q_PER_KV * blk
    nk = 3 * blk + kx_ref.shape[1]
    qi = lax.broadcasted_iota(jnp.int32, (nq, nk), 0) & (blk - 1)
    kj = lax.broadcasted_iota(jnp.int32, (nq, nk), 1)
    off_prev = jnp.where(n > 0, 0, 4 * blk)
    off_next = jnp.where(n < nb - 1, 0, 4 * blk)
    bad_prev = (kj < blk) & (kj < qi + off_prev)
    bad_next = (kj >= 2 * blk) & (kj < 3 * blk) & (kj - 2 * blk > qi - off_next)
    bad = bad_prev | bad_next
    rr = lax.broadcasted_iota(jnp.int32, (nq, 1), 0) // blk
    heads = range(ATTN_HEADS_PER_STEP)

    def head_cols(ref, i):
        return ref[0, :, i * HEAD_DIM:(i + 1) * HEAD_DIM]

    scores = []
    for i in heads:
        q = q_ref[0, :, i * Q_PER_KV * HEAD_DIM:(i + 1) * Q_PER_KV * HEAD_DIM]
        qs = jnp.concatenate([q[:, r * HEAD_DIM:(r + 1) * HEAD_DIM] for r in range(Q_PER_KV)], axis=0)
        kcat = jnp.concatenate([head_cols(kp_ref, i), head_cols(kc_ref, i), head_cols(kn_ref, i),
                                head_cols(kx_ref, i)], axis=0)
        scores.append(lax.dot_general(qs, kcat, (((1,), (1,)), ((), ())), preferred_element_type=F32))
    probs = []
    for i in heads:
        s = jnp.where(bad, NEG_INF, scores[i] * (HEAD_DIM ** -0.5))
        sk = jnp.zeros((nq, 1), F32)
        for r in range(Q_PER_KV):
            sk = jnp.where(rr == r, sink_ref[(hg * ATTN_HEADS_PER_STEP + i) * Q_PER_KV + r], sk)
        mx = jnp.maximum(jnp.max(s, axis=1, keepdims=True), sk)
        e = jnp.exp(s - mx)
        den = jnp.sum(e, axis=1, keepdims=True) + jnp.exp(sk - mx)
        probs.append((e / den).astype(BF16))
    for i in heads:
        vcat = jnp.concatenate([head_cols(vp_ref, i), head_cols(vc_ref, i), head_cols(vn_ref, i),
                                head_cols(vx_ref, i)], axis=0)
        o = jnp.dot(probs[i], vcat, preferred_element_type=F32)
        for r in range(Q_PER_KV):
            c0 = (i * Q_PER_KV + r) * HEAD_DIM
            o_ref[0, :, c0:c0 + HEAD_DIM] = o[r * blk:(r + 1) * blk, :].astype(o_ref.dtype)


def _attention(q, k, v, kc, vc, sink):
    b, s, _ = q.shape
    c = kc.shape[1]
    nb = s // ATTN_BLOCK
    qw = ATTN_HEADS_PER_STEP * Q_PER_KV * HEAD_DIM
    kw = ATTN_HEADS_PER_STEP * HEAD_DIM
    kv_spec = lambda f: pl.BlockSpec((1, ATTN_BLOCK, kw), f)
    prev = lambda bi, h, n: (bi, jnp.maximum(n - 1, 0), h)
    cur = lambda bi, h, n: (bi, n, h)
    nxt = lambda bi, h, n: (bi, jnp.minimum(n + 1, nb - 1), h)
    ctx_spec = pl.BlockSpec((1, c, kw), lambda bi, h, n: (bi, 0, h))
    return pl.pallas_call(
        _attn_kernel,
        grid=(b, KV_HEADS // ATTN_HEADS_PER_STEP, nb),
        in_specs=[pl.BlockSpec(memory_space=pltpu.SMEM),
                  pl.BlockSpec((1, ATTN_BLOCK, qw), lambda bi, h, n: (bi, n, h)),
                  kv_spec(prev), kv_spec(cur), kv_spec(nxt),
                  kv_spec(prev), kv_spec(cur), kv_spec(nxt),
                  ctx_spec, ctx_spec],
        out_specs=pl.BlockSpec((1, ATTN_BLOCK, qw), lambda bi, h, n: (bi, n, h)),
        out_shape=jax.ShapeDtypeStruct((b, s, Q_W), BF16),
        compiler_params=_params("arbitrary", "arbitrary", "arbitrary"),
        name="window_attn",
    )(sink, q, k, k, k, v, v, v, kc, vc)


CONV_HALO = 8
CONV_LANES = 512


def _conv_kernel(prev_ref, cur_ref, next_ref, w_ref, b_ref, *out_refs, widths):
    r = pl.program_id(1)
    rows = cur_ref.shape[1]
    keep_prev = jnp.where(r > 0, 1.0, 0.0)
    keep_next = jnp.where(r < pl.num_programs(1) - 1, 1.0, 0.0)
    col = 0
    for o_ref, width in zip(out_refs, widths):
        for c0 in range(0, width, CONV_LANES):
            lanes = slice(col + c0, col + c0 + CONV_LANES)
            xw = jnp.concatenate([prev_ref[0, :, lanes] * keep_prev, cur_ref[0, :, lanes],
                                  next_ref[0, :, lanes] * keep_next], axis=0)
            acc = b_ref[:, lanes]
            for j in range(D_CONV):
                off = j - D_CONV // 2
                tap = xw if off == 0 else pltpu.roll(xw, (-off) % xw.shape[0], 0)
                acc = acc + tap[CONV_HALO:CONV_HALO + rows, :] * w_ref[j:j + 1, lanes]
            o_ref[0, :, c0:c0 + CONV_LANES] = (acc * jax.nn.sigmoid(acc)).astype(o_ref.dtype)
        col += width


def _conv_silu(xbc, w, bias, widths):
    b, l, c = xbc.shape
    rows = min(256, l)
    per = rows // CONV_HALO
    last = l // CONV_HALO - 1
    return pl.pallas_call(
        functools.partial(_conv_kernel, widths=widths),
        grid=(b, l // rows),
        in_specs=[pl.BlockSpec((1, CONV_HALO, c), lambda i, r: (i, jnp.maximum(r * per - 1, 0), 0)),
                  pl.BlockSpec((1, rows, c), lambda i, r: (i, r, 0)),
                  pl.BlockSpec((1, CONV_HALO, c), lambda i, r: (i, jnp.minimum((r + 1) * per, last), 0)),
                  pl.BlockSpec((D_CONV, c), lambda i, r: (0, 0)),
                  pl.BlockSpec((1, c), lambda i, r: (0, 0))],
        out_specs=[pl.BlockSpec((1, rows, wd), lambda i, r: (i, r, 0)) for wd in widths],
        out_shape=[jax.ShapeDtypeStruct((b, l, wd), F32) for wd in widths],
        compiler_params=_params("arbitrary", "arbitrary"),
        name="dwconv_silu",
    )(xbc, xbc, xbc, w, bias)


def _dt_kernel(dt_ref, bias_ref, alog_ref, dto_ref, acs_ref):
    li = lax.broadcasted_iota(jnp.int32, (CHUNK, CHUNK), 0)
    si = lax.broadcasted_iota(jnp.int32, (CHUNK, CHUNK), 1)
    lower = jnp.where(li >= si, 1.0, 0.0).astype(F32)
    upper = jnp.where(li <= si, 1.0, 0.0).astype(F32)
    lane = lax.broadcasted_iota(jnp.int32, (1, DT_W), 1)
    neg_a = -jnp.exp(alog_ref[...])
    for k in range(dt_ref.shape[1] // CHUNK):
        rows = slice(k * CHUNK, (k + 1) * CHUNK)
        raw = dt_ref[0, rows, :] + bias_ref[...]
        dt = jnp.maximum(raw, 0.0) + jnp.log1p(jnp.exp(-jnp.abs(raw)))
        a = dt * neg_a
        pre = jnp.dot(lower, a, preferred_element_type=F32, precision=HIGHEST)
        suf = jnp.dot(upper, a, preferred_element_type=F32, precision=HIGHEST)
        dto_ref[0, rows, :] = dt
        acs_ref[0, rows, :] = jnp.where(lane < SSM_HEADS, pre, suf)


def _dt_prep(dt_raw, dt_bias, a_log):
    b, l, _ = dt_raw.shape
    rows = min(4 * CHUNK, l)
    spec = pl.BlockSpec((1, rows, DT_W), lambda i, c: (i, c, 0))
    vec = pl.BlockSpec((1, DT_W), lambda i, c: (0, 0))
    return pl.pallas_call(
        _dt_kernel,
        grid=(b, l // rows),
        in_specs=[spec, vec, vec],
        out_specs=[spec, spec],
        out_shape=[jax.ShapeDtypeStruct((b, l, DT_W), F32)] * 2,
        compiler_params=_params("arbitrary", "arbitrary"),
        name="dt_prep",
    )(dt_raw, dt_bias.reshape(1, DT_W), a_log.reshape(1, DT_W))


F32_PIECES = 3
EXPAND_K = F32_PIECES * HEADS_PER_GROUP
GROUPS_PER_TRIP = 2


def _group_layouts(t, direction):
    b, l, _ = t.shape
    t = t[..., direction * SSM_HEADS:(direction + 1) * SSM_HEADS].reshape(b, l, SSM_GROUPS, HEADS_PER_GROUP)
    cols = jnp.tile(jnp.transpose(t, (0, 2, 1, 3)), (1, 1, 1, F32_PIECES))
    return cols, jnp.transpose(t, (0, 2, 3, 1))


def _expand_heads(v3, width):
    lane = lax.broadcasted_iota(jnp.int32, (1, EXPAND_K), 1)
    hi = v3.astype(BF16).astype(F32)
    rest = v3 - hi
    mid = rest.astype(BF16).astype(F32)
    pieces = jnp.where(lane < HEADS_PER_GROUP, hi,
                       jnp.where(lane < 2 * HEADS_PER_GROUP, mid, rest - mid)).astype(BF16)
    shift = width.bit_length() - 1
    r = lax.broadcasted_iota(jnp.int32, (EXPAND_K, HEADS_PER_GROUP * width), 0)
    c = lax.broadcasted_iota(jnp.int32, (EXPAND_K, HEADS_PER_GROUP * width), 1)
    spread = jnp.where((r & (HEADS_PER_GROUP - 1)) == (c >> shift), 1.0, 0.0).astype(BF16)
    return jnp.dot(pieces, spread, preferred_element_type=F32)


def _ssd_kernel(*refs, reverse, emit_y, add_prev):
    it = iter(refs)
    x_ref, b_ref = next(it), next(it)
    c_ref = next(it) if emit_y else None
    dtc_ref, acsc_ref = next(it), next(it)
    acsr_ref = next(it) if emit_y else None
    h0_ref = next(it)
    if add_prev:
        yp_ref, dexp_ref, z_ref, gn_ref = next(it), next(it), next(it), next(it)
    y_ref = next(it) if emit_y else None
    hf_ref = next(it)
    hs_ref = next(it)
    ybuf_ref = next(it) if add_prev else None

    @pl.when(pl.program_id(1) == 0)
    def _():
        hs_ref[...] = h0_ref[0]

    edge = 0 if reverse else CHUNK - 1
    li = lax.broadcasted_iota(jnp.int32, (CHUNK, CHUNK), 0)
    si = lax.broadcasted_iota(jnp.int32, (CHUNK, CHUNK), 1)
    causal = (li <= si) if reverse else (li >= si)
    lo = lax.broadcasted_iota(jnp.int32, (1, 2 * SSM_HEADDIM), 1) < SSM_HEADDIM

    def load(g):
        lanes = pl.ds(pl.multiple_of(g * GROUP_W, GROUP_W), GROUP_W)
        nl = pl.ds(pl.multiple_of(g * D_STATE, D_STATE), D_STATE)
        v = dict(g=g, lanes=lanes, nl=nl, xg=x_ref[0, :, lanes], bmat=b_ref[0, :, nl],
                 acs3=acsc_ref[0, g], dt3=dtc_ref[0, g], hs=hs_ref[g])
        if emit_y:
            v.update(cmat=c_ref[0, :, nl], acs_r=acsr_ref[0, g])
        if add_prev:
            v.update(yp=yp_ref[0, :, lanes], dexp=dexp_ref[:, lanes])
        return v

    def stage1(v):
        acs3 = v["acs3"]
        last3 = acs3[edge:edge + 1, :]
        v["eacs_x"] = _expand_heads(jnp.exp(acs3), SSM_HEADDIM)
        v["dt_x"] = _expand_heads(v["dt3"], SSM_HEADDIM)
        v["wend_x"] = _expand_heads(jnp.exp(last3 - acs3), SSM_HEADDIM)
        if emit_y:
            c16 = v["cmat"].astype(BF16)
            v["col_x"] = _expand_heads(acs3, CHUNK)
            v["cbm"] = lax.dot_general(c16, v["bmat"].astype(BF16), (((1,), (1,)), ((), ())),
                                       preferred_element_type=F32)
            v["y_off"] = jnp.dot(c16, v["hs"].astype(BF16), preferred_element_type=F32)

    def stage2(v):
        g = v["g"]
        wx = v["xg"] * v["dt_x"]
        wxe = wx * v["wend_x"]
        states = jnp.dot(v["bmat"].T.astype(BF16), wxe.astype(BF16), preferred_element_type=F32)
        hs_new = v["eacs_x"][edge:edge + 1, :] * v["hs"] + states
        hs_ref[g] = hs_new
        hf_ref[0, g] = hs_new
        v["wx16"] = wx.astype(BF16)

    def stage3(v):
        parts = []
        for i in range(HEADS_PER_GROUP // 2):
            ms = []
            for j in (2 * i, 2 * i + 1):
                seg = v["col_x"][:, j * CHUNK:(j + 1) * CHUNK] - v["acs_r"][j:j + 1, :]
                dec = jnp.exp(jnp.where(causal, seg, -jnp.inf))
                ms.append((v["cbm"] * dec).astype(BF16))
            wp = v["wx16"][:, i * 128:(i + 1) * 128]
            zero = jnp.zeros_like(wp)
            rhs = jnp.concatenate([jnp.where(lo, wp, zero), jnp.where(lo, zero, wp)], axis=0)
            parts.append(jnp.dot(jnp.concatenate(ms, axis=1), rhs, preferred_element_type=F32))
        y = jnp.concatenate(parts, axis=1) + v["y_off"] * v["eacs_x"]
        if add_prev:
            ybuf_ref[:, v["lanes"]] = (v["yp"] + y) + v["dexp"] * v["xg"]
        else:
            y_ref[0, :, v["lanes"]] = y

    def group_batch(p, carry):
        vs = [load(GROUPS_PER_TRIP * p + k) for k in range(GROUPS_PER_TRIP)]
        for stage in (stage1, stage2) + ((stage3,) if emit_y else ()):
            for v in vs:
                stage(v)
        return carry

    lax.fori_loop(0, SSM_GROUPS // GROUPS_PER_TRIP, group_batch, 0)

    if add_prev:
        z = z_ref[0]
        t = ybuf_ref[...] * (z * jax.nn.sigmoid(z))
        ms = jnp.mean(t * t, axis=-1, keepdims=True)
        y_ref[0] = ((t * lax.rsqrt(ms + EPS)) * gn_ref[...]).astype(y_ref.dtype)


def _ssd_pass(x, bm, cm, dt, acs, h0, direction, emit_y, y_prev=None, d_exp=None, z=None, g_norm=None):
    b, l, _ = x.shape
    nc = l // CHUNK
    reverse = direction == 1
    add_prev = y_prev is not None
    dtc, _ = _group_layouts(dt, direction)
    acsc, acsr = _group_layouts(acs, direction)
    cidx = (lambda c: nc - 1 - c) if reverse else (lambda c: c)
    xspec = pl.BlockSpec((1, CHUNK, D_INNER), lambda i, c: (i, cidx(c), 0))
    bspec = pl.BlockSpec((1, CHUNK, GN_W), lambda i, c: (i, cidx(c), 0))
    colspec = pl.BlockSpec((1, SSM_GROUPS, CHUNK, EXPAND_K), lambda i, c: (i, 0, cidx(c), 0))
    rowspec = pl.BlockSpec((1, SSM_GROUPS, HEADS_PER_GROUP, CHUNK), lambda i, c: (i, 0, 0, cidx(c)))
    hspec = pl.BlockSpec((1, SSM_GROUPS, D_STATE, GROUP_W), lambda i, c: (i, 0, 0, 0))
    in_specs, args = [xspec, bspec], [x, bm]
    if emit_y:
        in_specs.append(bspec)
        args.append(cm)
    in_specs += [colspec, colspec]
    args += [dtc, acsc]
    if emit_y:
        in_specs.append(rowspec)
        args.append(acsr)
    in_specs.append(hspec)
    args.append(h0)
    scratch = [pltpu.VMEM((SSM_GROUPS, D_STATE, GROUP_W), F32)]
    if add_prev:
        vec = pl.BlockSpec((1, D_INNER), lambda i, c: (0, 0))
        in_specs += [xspec, vec, xspec, vec]
        args += [y_prev, d_exp, z, g_norm.reshape(1, D_INNER)]
        scratch.append(pltpu.VMEM((CHUNK, D_INNER), F32))
    out_specs, out_shape = [], []
    if emit_y:
        out_specs.append(xspec)
        out_shape.append(jax.ShapeDtypeStruct((b, l, D_INNER), BF16 if add_prev else F32))
    out_specs.append(hspec)
    out_shape.append(jax.ShapeDtypeStruct((b, SSM_GROUPS, D_STATE, GROUP_W), F32))
    res = pl.pallas_call(
        functools.partial(_ssd_kernel, reverse=reverse, emit_y=emit_y, add_prev=add_prev),
        grid=(b, nc),
        in_specs=in_specs,
        out_specs=out_specs,
        out_shape=out_shape,
        scratch_shapes=scratch,
        compiler_params=_params("arbitrary", "arbitrary"),
        name="ssd_scan",
    )(*args)
    return (res[0], res[1]) if emit_y else (None, res[0])


def _merge_kernel(yn_ref, at_ref, ws_ref, wa_ref, g1_ref, g2_ref, o_ref):
    t1 = jnp.dot(yn_ref[...], ws_ref[...], preferred_element_type=F32)
    t2 = jnp.dot(at_ref[...], wa_ref[...], preferred_element_type=F32)
    o_ref[...] = (jax.nn.sigmoid(g1_ref[...]) * t1 + jax.nn.sigmoid(g2_ref[...]) * t2).astype(o_ref.dtype)


def _merge(yn, attn, w_ssm, w_attn, gates):
    m = yn.shape[0]
    tm = min(512, m)
    tn = 1024
    nj = D_MODEL // tn
    return pl.pallas_call(
        _merge_kernel,
        grid=(m // tm, nj),
        in_specs=[pl.BlockSpec((tm, D_INNER), lambda i, j: (i, 0)),
                  pl.BlockSpec((tm, Q_W), lambda i, j: (i, 0)),
                  pl.BlockSpec((D_INNER, tn), lambda i, j: (0, j)),
                  pl.BlockSpec((Q_W, tn), lambda i, j: (0, j)),
                  pl.BlockSpec((tm, tn), lambda i, j: (i, j)),
                  pl.BlockSpec((tm, tn), lambda i, j: (i, nj + j))],
        out_specs=pl.BlockSpec((tm, tn), lambda i, j: (i, j)),
        out_shape=jax.ShapeDtypeStruct((m, D_MODEL), BF16),
        compiler_params=_params("arbitrary", "arbitrary"),
        name="branch_merge",
    )(yn, attn, w_ssm, w_attn, gates, gates)


def _out_proj_kernel(a_ref, w_ref, x_ref, g_ref, o_ref):
    mix = jnp.dot(a_ref[0], w_ref[...], preferred_element_type=F32)
    o_ref[0] = x_ref[0] + g_ref[0] * mix


def _out_proj(merged, w_out, x, gate):
    b, s, d = x.shape
    tm = min(1024, s)
    tn = 1024
    return pl.pallas_call(
        _out_proj_kernel,
        grid=(b, s // tm, d // tn),
        in_specs=[pl.BlockSpec((1, tm, d), lambda i, r, j: (i, r, 0)),
                  pl.BlockSpec((d, tn), lambda i, r, j: (0, j)),
                  pl.BlockSpec((1, tm, tn), lambda i, r, j: (i, r, j)),
                  pl.BlockSpec((1, 1, tn), lambda i, r, j: (i, 0, j))],
        out_specs=pl.BlockSpec((1, tm, tn), lambda i, r, j: (i, r, j)),
        out_shape=jax.ShapeDtypeStruct((b, s, d), F32),
        compiler_params=_params("arbitrary", "arbitrary", "arbitrary"),
        name="out_proj",
    )(merged, w_out, x, gate)


PEER_TT = 256


def _topk_rounds(s, n_rounds, break_ties):
    rows = s.shape[0]
    rid = lax.broadcasted_iota(jnp.int32, s.shape, 0).astype(F32)
    rank = jnp.full(s.shape, float(n_rounds), F32)
    vals = []
    for r in range(n_rounds):
        m = jnp.max(s, axis=0, keepdims=True)
        hit = s == m
        if break_ties:
            hit = rid == jnp.min(jnp.where(hit, rid, float(rows)), axis=0, keepdims=True)
        rank = jnp.where(hit, float(r), rank)
        s = jnp.where(hit, -jnp.inf, s)
        vals.append(m)
    return jnp.concatenate(vals, axis=0), rank


def _topk_rows(scores, n_rounds, scratch):
    worst = None
    for s, (v_scr, r_scr) in zip(scores, scratch):
        v_scr[...], r_scr[...] = _topk_rounds(s, n_rounds, False)
        taken = jnp.sum(jnp.where(r_scr[...] < float(n_rounds), 1.0, 0.0), axis=0, keepdims=True)
        worst = taken if worst is None else jnp.maximum(worst, taken)

    @pl.when(jnp.max(worst) > float(n_rounds))
    def _():
        for s, (v_scr, r_scr) in zip(scores, scratch):
            v_scr[...], r_scr[...] = _topk_rounds(s, n_rounds, True)

    return [(v_scr[...], r_scr[...]) for v_scr, r_scr in scratch]


def _staircase_counts(v1, v2):
    k = v1.shape[0]
    aid = lax.broadcasted_iota(jnp.int32, v1.shape, 0).astype(F32)
    n = jnp.zeros_like(v1)
    front = v1 + v2[0:1, :]
    picked = []
    for _ in range(k):
        m = jnp.max(front, axis=0, keepdims=True)
        idx = jnp.min(jnp.where(front == m, aid, float(k)), axis=0, keepdims=True)
        hit = aid == idx
        n = jnp.where(hit, n + 1.0, n)
        nxt = jnp.full_like(v1, -jnp.inf)
        for b in range(1, k):
            nxt = jnp.where(n == float(b), v2[b:b + 1, :], nxt)
        front = jnp.where(hit, v1 + nxt, front)
        picked.append(m)
    return n, jnp.concatenate(picked, axis=0)


PEER_HEADS_PER_TRIP = 4


def _peer_topk_kernel(u_ref, wq_ref, k1_ref, k2_ref, e1_ref, nc_ref, e2_ref, rb_ref, q_scr, *vr_scr):
    q_scr[...] = jnp.dot(u_ref[...], wq_ref[...], preferred_element_type=F32)
    scratch = tuple((vr_scr[2 * i], vr_scr[2 * i + 1]) for i in range(len(vr_scr) // 2))

    def scores(h):
        base = pl.multiple_of(h * 2 * KEY_HALF, 2 * KEY_HALF)
        q1 = q_scr[:, pl.ds(base, KEY_HALF)].astype(BF16)
        q2 = q_scr[:, pl.ds(pl.multiple_of(base + KEY_HALF, KEY_HALF), KEY_HALF)].astype(BF16)
        nt = (((1,), (1,)), ((), ()))
        return (lax.dot_general(k1_ref[...], q1, nt, preferred_element_type=F32),
                lax.dot_general(k2_ref[...], q2, nt, preferred_element_type=F32))

    def trip(p, carry):
        hs = [p * PEER_HEADS_PER_TRIP + i for i in range(PEER_HEADS_PER_TRIP)]
        sc = [scores(h) for h in hs]
        picks = _topk_rows([s for pair in sc for s in pair], PEER_TOPK, scratch)
        for i, h in enumerate(hs):
            finish(h, sc[i][0], sc[i][1], *picks[2 * i], *picks[2 * i + 1])
        return carry

    def finish(h, sc1, sc2, v1, r1, v2, r2):
        n, top = _staircase_counts(v1, v2)
        z = jnp.sum(jnp.exp(top - top[0:1, :]), axis=0, keepdims=True)
        nc = jnp.zeros_like(sc1)
        for a in range(PEER_TOPK):
            nc = jnp.where(r1 == float(a), n[a:a + 1, :], nc)
        e1_ref[h] = jnp.where(r1 < float(PEER_TOPK), jnp.exp(sc1 - v1[0:1, :]) / z, 0.0)
        nc_ref[h] = nc
        e2_ref[h] = jnp.exp(sc2 - v2[0:1, :]).astype(e2_ref.dtype)
        rb_ref[h] = r2.astype(rb_ref.dtype)

    lax.fori_loop(0, PEER_HEADS // PEER_HEADS_PER_TRIP, trip, 0)


def _peer_topk(u2, wq, keys1, keys2):
    m, d = u2.shape
    tt = min(PEER_TT, m)
    big = pl.BlockSpec((PEER_HEADS, N_KEYS, tt), lambda i: (0, 0, i))
    bshape = jax.ShapeDtypeStruct((PEER_HEADS, N_KEYS, m), F32)
    bshape16 = jax.ShapeDtypeStruct((PEER_HEADS, N_KEYS, m), BF16)
    kspec = pl.BlockSpec((N_KEYS, KEY_HALF), lambda i: (0, 0))
    return pl.pallas_call(
        _peer_topk_kernel,
        grid=(m // tt,),
        in_specs=[pl.BlockSpec((tt, d), lambda i: (i, 0)),
                  pl.BlockSpec((d, PEER_HEADS * 2 * KEY_HALF), lambda i: (0, 0)),
                  kspec, kspec],
        out_specs=[big, big, big, big],
        out_shape=[bshape, bshape, bshape16, bshape16],
        scratch_shapes=[pltpu.VMEM((tt, PEER_HEADS * 2 * KEY_HALF), F32)]
        + [pltpu.VMEM((PEER_TOPK, tt), F32), pltpu.VMEM((N_KEYS, tt), F32)] * (2 * PEER_HEADS_PER_TRIP),
        compiler_params=_params("arbitrary"),
        name="peer_topk",
    )(u2, wq, keys1, keys2)


PEER_NI = 8
PEER_DENSE_TT = 512
PEER_LANES = 256
SQRT_HALF = float(math.sqrt(0.5))


def _peer_stage(tile, lanes, ut_ref, pu_ref, pvt_ref, e1_ref, nc_ref, e2_ref, rb_ref, acc_ref, ht_w, ht_r):
    ht_w[:, lanes] = lax.dot_general(pu_ref[...], ut_ref[lanes, :], (((1,), (1,)), ((), ())),
                                     preferred_element_type=F32)
    gas = []
    for ii in range(PEER_NI):
        i1 = tile * PEER_NI + ii
        hb = ht_r[ii * N_KEYS:(ii + 1) * N_KEYS, lanes]
        act = 0.5 * hb * (1.0 + lax.erf(hb * SQRT_HALF))
        g = jnp.zeros(hb.shape, BF16)
        for h in range(PEER_HEADS):
            e1c = e1_ref[h, pl.ds(i1, 1), lanes].astype(BF16)
            ncc = nc_ref[h, pl.ds(i1, 1), lanes].astype(BF16)
            g = g + jnp.where(rb_ref[h, :, lanes] < ncc, e1c * e2_ref[h, :, lanes], jnp.zeros((), BF16))
        gas.append(g * act.astype(BF16))
    acc_ref[:, lanes] += jnp.dot(pvt_ref[0], jnp.concatenate(gas, axis=0), preferred_element_type=F32)


def _peer_dense_kernel(ut_ref, pu_ref, pvt_ref, e1_ref, nc_ref, e2_ref, rb_ref,
                       h_ref, g5_ref, gf_ref, o_ref, acc_ref, ht0_ref, ht1_ref):
    e = pl.program_id(1)
    ne = pl.num_programs(1)
    part = pl.program_id(2)
    tile = jnp.maximum(e - 1, 0)

    for k in range(ut_ref.shape[0] // PEER_LANES):
        lanes = slice(k * PEER_LANES, (k + 1) * PEER_LANES)
        stage = functools.partial(_peer_stage, tile, lanes, ut_ref, pu_ref, pvt_ref, e1_ref, nc_ref, e2_ref,
                                  rb_ref, acc_ref)

        @pl.when((part == k) & (e == 0))
        def _():
            acc_ref[:, lanes] = jnp.zeros((acc_ref.shape[0], PEER_LANES), F32)
            ht1_ref[:, lanes] = jnp.zeros((ht1_ref.shape[0], PEER_LANES), F32)

        @pl.when((part == k) & (e % 2 == 0))
        def _():
            stage(ht0_ref, ht1_ref)

        @pl.when((part == k) & (e % 2 == 1))
        def _():
            stage(ht1_ref, ht0_ref)

        @pl.when((part == k) & (e == ne - 1))
        def _():
            hres = h_ref[0, lanes, :] + g5_ref[0] * acc_ref[:, lanes].T
            ms = jnp.mean(hres * hres, axis=-1, keepdims=True)
            o_ref[0, lanes, :] = (hres * lax.rsqrt(ms + EPS)) * gf_ref[...]


def _peer_dense(u2, pu, pvt, e1, nc, e2, rb, h1, gate5, gfinal):
    b, s, d = h1.shape
    m = b * s
    tt = min(PEER_DENSE_TT, s)
    per = s // tt
    te = PEER_NI * N_KEYS
    ne = N_EXPERTS // te
    once = pl.Buffered(1)
    big = pl.BlockSpec((PEER_HEADS, N_KEYS, tt), lambda i, e, k: (0, 0, i), pipeline_mode=once)
    return pl.pallas_call(
        _peer_dense_kernel,
        grid=(m // tt, ne + 1, tt // PEER_LANES),
        in_specs=[pl.BlockSpec((tt, d), lambda i, e, k: (i, 0), pipeline_mode=once),
                  pl.BlockSpec((te, d), lambda i, e, k: (jnp.minimum(e, ne - 1), 0)),
                  pl.BlockSpec((1, d, te), lambda i, e, k: (jnp.maximum(e - 1, 0), 0, 0)),
                  big, big, big, big,
                  pl.BlockSpec((1, tt, d), lambda i, e, k: (i // per, i % per, 0), pipeline_mode=once),
                  pl.BlockSpec((1, 1, d), lambda i, e, k: (i // per, 0, 0)),
                  pl.BlockSpec((1, d), lambda i, e, k: (0, 0))],
        out_specs=pl.BlockSpec((1, tt, d), lambda i, e, k: (i // per, i % per, 0)),
        out_shape=jax.ShapeDtypeStruct((b, s, d), F32),
        scratch_shapes=[pltpu.VMEM((d, tt), F32), pltpu.VMEM((te, tt), F32), pltpu.VMEM((te, tt), F32)],
        compiler_params=_params("arbitrary", "arbitrary", "arbitrary", vmem=VMEM_LIMIT_PEER_BYTES),
        name="peer_dense",
    )(u2, pu, pvt, e1, nc, e2, rb, h1, gate5, gfinal.reshape(1, d))


def _transpose_cast_kernel(w_ref, o_ref):
    half = w_ref.shape[0] // 2
    for k in range(2):
        o_ref[0, :, k * half:(k + 1) * half] = w_ref[k * half:(k + 1) * half, :].T.astype(o_ref.dtype)


def _transpose_cast(w, te):
    e, d = w.shape
    return pl.pallas_call(
        _transpose_cast_kernel,
        grid=(e // te,),
        in_specs=[pl.BlockSpec((te, d), lambda i: (i, 0))],
        out_specs=pl.BlockSpec((1, d, te), lambda i: (i, 0, 0)),
        out_shape=jax.ShapeDtypeStruct((e // te, d, te), BF16),
        compiler_params=_params("arbitrary"),
        name="transpose_cast",
    )(w)


def _rope_tables(rows):
    row = jnp.repeat(jnp.arange(rows), GRID_W).astype(F32)
    col = jnp.tile(jnp.arange(GRID_W), rows).astype(F32)
    freqs = ROPE_BASE ** (-jnp.arange(ROPE_PAIRS, dtype=F32) / ROPE_PAIRS)
    ar = row[:, None] * freqs
    ac = col[:, None] * freqs
    ang = jnp.concatenate([ar, ar, ac, ac], axis=-1)
    cos, sin = jnp.cos(ang), jnp.sin(ang)
    lane = jnp.arange(HEAD_DIM)
    first = (lane % (2 * ROPE_PAIRS)) < ROPE_PAIRS
    return cos, jnp.where(first, -sin, 0.0), jnp.where(first, 0.0, sin)


def kernel(x, c, ctx, c_ctx, ada_w, ada_b, norm1_g, w_in, conv_w, conv_b, dt_bias, a_log, ssm_d, ssm_norm_g,
           attn_sink, w_branch_ssm, w_branch_attn, w_out, norm2_g, peer_wq, peer_keys1, peer_keys2, peer_u,
           peer_v, final_norm_g):
    b, s, d = x.shape
    cl = ctx.shape[1]
    m = b * s
    l = 0

    cc = jnp.concatenate([c, c_ctx[None, :], jnp.zeros((8 - b - 1, d), F32)], axis=0)
    mod_all = _ada(cc, ada_w[l], ada_b[l][None, :])
    mod = mod_all[:b].reshape(b, 6, 1, d)
    modc = jnp.broadcast_to(mod_all[b].reshape(1, 6, 1, d), (b, 6, 1, d))

    u = _norm_mod(x, norm1_g[l], mod[:, 1], mod[:, 0]).reshape(m, d)
    uc = _norm_mod(ctx, norm1_g[l], modc[:, 1], modc[:, 0]).reshape(b * cl, d)

    wi = w_in[l]
    o = 0
    w_k = wi[:, o:o + K_W].astype(BF16); o += K_W
    w_v = wi[:, o:o + K_W].astype(BF16); o += K_W
    w_xbc = wi[:, o:o + XBC_W].astype(BF16); o += XBC_W
    w_dt = wi[:, o:o + DT_W].astype(BF16); o += DT_W
    w_q = wi[:, o:o + Q_W].astype(BF16); o += Q_W
    w_z = wi[:, o:o + Z_W].astype(BF16); o += Z_W
    w_g = wi[:, o:o + GATE_W].astype(BF16)

    rope = _rope_tables(s // GRID_W)
    q = _mm(u, w_q, BF16, rope=rope, seq=s).reshape(b, s, Q_W)
    k = _mm(u, w_k, BF16, rope=rope, seq=s).reshape(b, s, K_W)
    v = _mm(u, w_v, BF16).reshape(b, s, K_W)
    xbc = _mm(u, w_xbc, F32).reshape(b, s, XBC_W)
    dt_raw = _mm(u, w_dt, F32).reshape(b, s, DT_W)
    z = _mm(u, w_z, F32)
    gates = _mm(u, w_g, F32)
    kc = _mm(uc, w_k, BF16).reshape(b, cl, K_W)
    vc = _mm(uc, w_v, BF16).reshape(b, cl, K_W)
    xbcc = _mm(uc, w_xbc, F32).reshape(b, cl, XBC_W)
    dtc_raw = _mm(uc, w_dt, F32).reshape(b, cl, DT_W)

    attn = _attention(q, k, v, kc, vc, attn_sink[l]).reshape(m, Q_W)

    cw, cb = conv_w[l], conv_b[l][None, :]
    xs, bm, cm = _conv_silu(xbc, cw, cb, (D_INNER, GN_W, GN_W))
    xsc, bmc = _conv_silu(xbcc, cw, cb, (D_INNER, GN_W))
    dt, acs = _dt_prep(dt_raw, dt_bias[l], a_log[l])
    dtc, acsc = _dt_prep(dtc_raw, dt_bias[l], a_log[l])
    hzero = jnp.zeros((b, SSM_GROUPS, D_STATE, GROUP_W), F32)
    _, hcf = _ssd_pass(xsc, bmc, None, dtc, acsc, hzero, 0, False)
    _, hcb = _ssd_pass(xsc, bmc, None, dtc, acsc, hzero, 1, False)
    y_f, _ = _ssd_pass(xs, bm, cm, dt, acs, hcf, 0, True)
    d_exp = jnp.repeat(ssm_d[l], SSM_HEADDIM)[None, :]
    yn, _ = _ssd_pass(xs, bm, cm, dt, acs, hcb, 1, True, y_prev=y_f, d_exp=d_exp,
                      z=z.reshape(b, s, Z_W), g_norm=ssm_norm_g[l])
    yn = yn.reshape(m, D_INNER)
    merged = _merge(yn, attn, w_branch_ssm[l].astype(BF16), w_branch_attn[l].astype(BF16), gates)
    h1 = _out_proj(merged.reshape(b, s, d), w_out[l].astype(BF16), x, mod[:, 2])

    u2 = _norm_mod(h1, norm2_g[l], mod[:, 4], mod[:, 3]).reshape(m, d)
    e1, nc, e2, rb = _peer_topk(u2, peer_wq[l].astype(BF16),
                                peer_keys1[l].astype(BF16), peer_keys2[l].astype(BF16))
    pu = peer_u[l].astype(BF16)
    pvt = _transpose_cast(peer_v[l], PEER_NI * N_KEYS)
    return _peer_dense(u2, pu, pvt, e1, nc, e2, rb, h1, mod[:, 5], final_norm_g)
```
